```python
import math
import jax, jax.numpy as jnp
from jax import lax
import numpy as np

D_MODEL = 1024
BATCH = 32
SEQ = 256
DEPTH = 4
DEC_BATCH = 4
DEC_SEQ = 2048
PAST_LEN = 512

GRID_W = 64
N_MIXERS = 3
N_CONV_LAYERS = (DEPTH + 2) // 3
N_SWA_LAYERS = (DEPTH + 1) // 3
N_DIFF_LAYERS = DEPTH // 3
CONV_WIDTH = 3
HEAD_DIM = 64
SWA_Q_HEADS = 16
SWA_KV_HEADS = 4
SWA_WINDOW = 128
BLOCK = 128
DIFF_HEADS = 8
N_EXPERTS = 32
TOP_K = 4
D_EXPERT = 1024
SWIGLU_ALPHA = 1.702
SWIGLU_LIMIT = 7.0
ROPE_BASE = 10000.0
EPS = 1e-6

kernel_name = "hybrid_diffusion_conv_swa_diffattn_moe_step"

F32 = jnp.float32


def rms_norm(x, g):
    xf = x.astype(F32)
    y = xf * lax.rsqrt(jnp.mean(xf * xf, axis=-1, keepdims=True) + EPS)
    return (y * g.astype(F32)).astype(x.dtype)


def modulation(cond, w, b):
    m = jax.nn.silu(cond) @ w + b
    return jnp.split(m[:, None, :], 6, axis=-1)


def axial_rope(n_tok):
    rows = n_tok // GRID_W
    row_pos = jnp.repeat(jnp.arange(rows, dtype=F32), GRID_W)
    col_pos = jnp.tile(jnp.arange(GRID_W, dtype=F32), rows)
    half = HEAD_DIM // 2
    inv_freq = ROPE_BASE ** (-jnp.arange(0, half, 2, dtype=F32) / half)
    ang_r = row_pos[:, None] * inv_freq[None, :]
    ang_c = col_pos[:, None] * inv_freq[None, :]
    return (jnp.cos(ang_r), jnp.sin(ang_r), jnp.cos(ang_c), jnp.sin(ang_c))


def _rotate(x, cos, sin):
    x1, x2 = jnp.split(x, 2, axis=-1)
    c = cos[:, None, :]
    s = sin[:, None, :]
    return jnp.concatenate([x1 * c - x2 * s, x2 * c + x1 * s], axis=-1)


def apply_axial_rope(x, rope):
    cr, sr, cc, sc = rope
    xr, xc = jnp.split(x.astype(F32), 2, axis=-1)
    return jnp.concatenate([_rotate(xr, cr, sr), _rotate(xc, cc, sc)], axis=-1).astype(x.dtype)


def short_conv_mixer(h, w_in, conv_w, w_out):
    gate_b, gate_c, v = jnp.split(h @ w_in, 3, axis=-1)
    u = gate_c * v
    conv = lax.conv_general_dilated(
        u, conv_w[:, None, :].astype(u.dtype), window_strides=(1,),
        padding=((CONV_WIDTH // 2, CONV_WIDTH // 2),),
        dimension_numbers=("NWC", "WIO", "NWC"), feature_group_count=u.shape[-1])
    return (gate_b * conv) @ w_out


def sink_softmax(s, sink):
    sk = sink.astype(F32).reshape(1, s.shape[1], s.shape[2], 1, 1)
    mx = jnp.maximum(jnp.max(s, axis=-1, keepdims=True), sk)
    p = jnp.exp(s - mx)
    return p / (jnp.sum(p, axis=-1, keepdims=True) + jnp.exp(sk - mx))


def swa_context_attention(q, k, v, sink):
    b, n, hq, dh = q.shape
    hkv = k.shape[2]
    g = hq // hkv
    scale = dh ** -0.5
    qb = q.reshape(b, n // BLOCK, BLOCK, hkv, g, dh).transpose(1, 0, 2, 3, 4, 5)

    def one(qblk):
        s = jnp.einsum("bqkgd,bmkd->bkgqm", qblk, k).astype(F32) * scale
        p = sink_softmax(s, sink)
        return jnp.einsum("bkgqm,bmkd->bqkgd", p.astype(v.dtype), v)

    out = lax.map(one, qb)
    return out.transpose(1, 0, 2, 3, 4, 5).reshape(b, n, hq * dh)


def swa_latent_attention(q, k, v, k_ctx, v_ctx, sink):
    b, n, hq, dh = q.shape
    hkv = k.shape[2]
    g = hq // hkv
    nb = n // BLOCK
    scale = dh ** -0.5
    pad = ((0, 0), (BLOCK, BLOCK), (0, 0), (0, 0))
    kp = jnp.pad(k, pad)
    vp = jnp.pad(v, pad)
    qb = q.reshape(b, nb, BLOCK, hkv, g, dh).transpose(1, 0, 2, 3, 4, 5)
    q_off = jnp.arange(BLOCK)[:, None]
    k_off = jnp.arange(3 * BLOCK)[None, :]
    rel = k_off - BLOCK - q_off

    def one(args):
        j, qblk = args
        kb = lax.dynamic_slice_in_dim(kp, j * BLOCK, 3 * BLOCK, axis=1)
        vb = lax.dynamic_slice_in_dim(vp, j * BLOCK, 3 * BLOCK, axis=1)
        key_pos = (j - 1) * BLOCK + k_off
        valid = (jnp.abs(rel) <= SWA_WINDOW) & (key_pos >= 0) & (key_pos < n)
        s_lat = jnp.einsum("bqkgd,bmkd->bkgqm", qblk, kb).astype(F32) * scale
        s_lat = jnp.where(valid, s_lat, -jnp.inf)
        s_ctx = jnp.einsum("bqkgd,bmkd->bkgqm", qblk, k_ctx).astype(F32) * scale
        p = sink_softmax(jnp.concatenate([s_lat, s_ctx], axis=-1), sink)
        vals = jnp.concatenate([vb, v_ctx], axis=1)
        return jnp.einsum("bkgqm,bmkd->bqkgd", p.astype(v.dtype), vals)

    out = lax.map(one, (jnp.arange(nb), qb))
    return out.transpose(1, 0, 2, 3, 4, 5).reshape(b, n, hq * dh)


def swa_mixer(h, w_qkv, q_gain, k_gain, sink, w_out, latent, rope, ctx_kv):
    b, n, _ = h.shape
    nq = SWA_Q_HEADS * HEAD_DIM
    nk = SWA_KV_HEADS * HEAD_DIM
    q, k, v = jnp.split(h @ w_qkv, [nq, nq + nk], axis=-1)
    q = rms_norm(q.reshape(b, n, SWA_Q_HEADS, HEAD_DIM), q_gain)
    k = rms_norm(k.reshape(b, n, SWA_KV_HEADS, HEAD_DIM), k_gain)
    v = v.reshape(b, n, SWA_KV_HEADS, HEAD_DIM)
    if not latent:
        return swa_context_attention(q, k, v, sink) @ w_out, (k, v)
    q = apply_axial_rope(q, rope)
    k = apply_axial_rope(k, rope)
    o = swa_latent_attention(q, k, v, ctx_kv[0], ctx_kv[1], sink)
    return o @ w_out, None


def diff_attention(q, k, v, lam):
    b, n, nh, _, dh = q.shape
    scale = dh ** -0.5
    qb = q.reshape(b, n // BLOCK, BLOCK, nh, 2, dh).transpose(1, 0, 2, 3, 4, 5)

    def one(qblk):
        s = jnp.einsum("bqhcd,bmhcd->bhcqm", qblk, k).astype(F32) * scale
        p = jax.nn.softmax(s, axis=-1)
        w = p[:, :, 0] - lam * p[:, :, 1]
        return jnp.einsum("bhqm,bmhe->bqhe", w.astype(v.dtype), v)

    out = lax.map(one, qb)
    return out.transpose(1, 0, 2, 3, 4).reshape(b, n, nh, 2 * dh)


def diff_mixer(h, w_qkv, q_gain, k_gain, lam_q1, lam_k1, lam_q2, lam_k2, sub_gain, w_out,
               lam_init, latent, rope, ctx_kv):
    b, n, _ = h.shape
    q, k, v = jnp.split(h @ w_qkv, 3, axis=-1)
    q = rms_norm(q.reshape(b, n, DIFF_HEADS, 2, HEAD_DIM), q_gain)
    k = rms_norm(k.reshape(b, n, DIFF_HEADS, 2, HEAD_DIM), k_gain)
    v = v.reshape(b, n, DIFF_HEADS, 2 * HEAD_DIM)
    lam = (jnp.exp(jnp.sum(lam_q1.astype(F32) * lam_k1.astype(F32)))
           - jnp.exp(jnp.sum(lam_q2.astype(F32) * lam_k2.astype(F32))) + lam_init)
    if not latent:
        new_kv = (k, v)
        o = diff_attention(q, k, v, lam)
    else:
        new_kv = None
        q = apply_axial_rope(q.reshape(b, n, DIFF_HEADS * 2, HEAD_DIM), rope).reshape(q.shape)
        k = apply_axial_rope(k.reshape(b, n, DIFF_HEADS * 2, HEAD_DIM), rope).reshape(k.shape)
        k_all = jnp.concatenate([k, ctx_kv[0]], axis=1)
        v_all = jnp.concatenate([v, ctx_kv[1]], axis=1)
        o = diff_attention(q, k_all, v_all, lam)
    o = rms_norm(o, sub_gain) * (1.0 - lam_init)
    return o.reshape(b, n, DIFF_HEADS * 2 * HEAD_DIM) @ w_out, new_kv


def moe(h, w_router, b_router, w_gate_up, b_gate_up, w_down, b_down):
    b, n, d = h.shape
    t = h.reshape(b * n, d)
    logits = (t @ w_router + b_router).astype(F32)
    top_val, top_idx = lax.top_k(logits, TOP_K)
    gates = jax.nn.softmax(top_val, axis=-1)
    combine = jnp.sum(jax.nn.one_hot(top_idx, N_EXPERTS, dtype=F32) * gates[..., None], axis=1)

    def expert(acc, params):
        wgu, bgu, wd, bd, cw = params
        gu = t @ wgu + bgu
        x_glu = jnp.minimum(gu[:, ::2], SWIGLU_LIMIT)
        x_lin = jnp.clip(gu[:, 1::2], -SWIGLU_LIMIT, SWIGLU_LIMIT)
        act = x_glu * jax.nn.sigmoid(SWIGLU_ALPHA * x_glu) * (x_lin + 1.0)
        y = act @ wd + bd
        return acc + cw[:, None].astype(y.dtype) * y, None

    out, _ = lax.scan(expert, jnp.zeros_like(t),
                      (w_gate_up, b_gate_up, w_down, b_down, combine.T))
    return out.reshape(b, n, d)


def setup_inputs(seed: int = 0) -> dict:
    key = jax.random.key(seed)
    ks = iter(jax.random.split(key, 40))

    def nrm(shape, scale=1.0):
        return jax.random.normal(next(ks), shape, jnp.float32) * scale

    D = D_MODEL
    n_swa_qkv = (SWA_Q_HEADS + 2 * SWA_KV_HEADS) * HEAD_DIM
    n_diff = DIFF_HEADS * 2 * HEAD_DIM
    return {
        "x_prompt": nrm((BATCH, SEQ, D)),
        "x_sample": nrm((DEC_BATCH, DEC_SEQ, D)),
        "c": nrm((DEC_BATCH, D)),
        "cache_swa_k": nrm((DEC_BATCH, N_SWA_LAYERS, PAST_LEN, SWA_KV_HEADS, HEAD_DIM)),
        "cache_swa_v": nrm((DEC_BATCH, N_SWA_LAYERS, PAST_LEN, SWA_KV_HEADS, HEAD_DIM)),
        "cache_diff_k": nrm((DEC_BATCH, N_DIFF_LAYERS, PAST_LEN, DIFF_HEADS, 2, HEAD_DIM)),
        "cache_diff_v": nrm((DEC_BATCH, N_DIFF_LAYERS, PAST_LEN, DIFF_HEADS, 2 * HEAD_DIM)),
        "c_ctx": nrm((D,)),
        "norm_mix_g": 1.0 + nrm((DEPTH, D), 0.02),
        "norm_ffn_g": 1.0 + nrm((DEPTH, D), 0.02),
        "ada_w": nrm((DEPTH, D, 6 * D), 0.5 * D ** -0.5),
        "ada_b": nrm((DEPTH, 6 * D), 0.02),
        "conv_w_in": nrm((N_CONV_LAYERS, D, 3 * D), D ** -0.5),
        "conv_w": nrm((N_CONV_LAYERS, CONV_WIDTH, D), CONV_WIDTH ** -0.5),
        "conv_w_out": nrm((N_CONV_LAYERS, D, D), D ** -0.5),
        "swa_w_qkv": nrm((N_SWA_LAYERS, D, n_swa_qkv), D ** -0.5),
        "swa_q_gain": 1.0 + nrm((N_SWA_LAYERS, HEAD_DIM), 0.02),
        "swa_k_gain": 1.0 + nrm((N_SWA_LAYERS, HEAD_DIM), 0.02),
        "swa_sink": nrm((N_SWA_LAYERS, SWA_Q_HEADS)),
        "swa_w_out": nrm((N_SWA_LAYERS, SWA_Q_HEADS * HEAD_DIM, D), (SWA_Q_HEADS * HEAD_DIM) ** -0.5),
        "diff_w_qkv": nrm((N_DIFF_LAYERS, D, 3 * n_diff), D ** -0.5),
        "diff_q_gain": 1.0 + nrm((N_DIFF_LAYERS, HEAD_DIM), 0.02),
        "diff_k_gain": 1.0 + nrm((N_DIFF_LAYERS, HEAD_DIM), 0.02),
        "diff_lam_q1": nrm((N_DIFF_LAYERS, HEAD_DIM), 0.1),
        "diff_lam_k1": nrm((N_DIFF_LAYERS, HEAD_DIM), 0.1),
        "diff_lam_q2": nrm((N_DIFF_LAYERS, HEAD_DIM), 0.1),
        "diff_lam_k2": nrm((N_DIFF_LAYERS, HEAD_DIM), 0.1),
        "diff_sub_gain": 1.0 + nrm((N_DIFF_LAYERS, 2 * HEAD_DIM), 0.02),
        "diff_w_out": nrm((N_DIFF_LAYERS, n_diff, D), n_diff ** -0.5),
        "moe_w_router": nrm((DEPTH, D, N_EXPERTS), D ** -0.5),
        "moe_b_router": nrm((DEPTH, N_EXPERTS), 0.01),
        "moe_w_gate_up": nrm((DEPTH, N_EXPERTS, D, 2 * D_EXPERT), D ** -0.5),
        "moe_b_gate_up": nrm((DEPTH, N_EXPERTS, 2 * D_EXPERT), 0.01),
        "moe_w_down": nrm((DEPTH, N_EXPERTS, D_EXPERT, D), D_EXPERT ** -0.5),
        "moe_b_down": nrm((DEPTH, N_EXPERTS, D), 0.01),
    }


def reference(x_prompt, x_sample, c, cache_swa_k, cache_swa_v, cache_diff_k, cache_diff_v,
              c_ctx, norm_mix_g, norm_ffn_g, ada_w, ada_b,
              conv_w_in, conv_w, conv_w_out,
              swa_w_qkv, swa_q_gain, swa_k_gain, swa_sink, swa_w_out,
              diff_w_qkv, diff_q_gain, diff_k_gain, diff_lam_q1, diff_lam_k1,
              diff_lam_q2, diff_lam_k2, diff_sub_gain, diff_w_out,
              moe_w_router, moe_b_router, moe_w_gate_up, moe_b_gate_up, moe_w_down, moe_b_down):

    def layer(i, x, cond, latent, rope, ctx_kv):
        sh1, sc1, g1, sh2, sc2, g2 = modulation(cond, ada_w[i], ada_b[i])
        h = rms_norm(x, norm_mix_g[i]) * (1.0 + sc1) + sh1
        j = i // N_MIXERS
        kind = i % N_MIXERS
        new_kv = None
        if kind == 0:
            mix = short_conv_mixer(h, conv_w_in[j], conv_w[j], conv_w_out[j])
        elif kind == 1:
            mix, new_kv = swa_mixer(h, swa_w_qkv[j], swa_q_gain[j], swa_k_gain[j], swa_sink[j],
                                    swa_w_out[j], latent, rope, ctx_kv)
        else:
            lam_init = 0.8 - 0.6 * math.exp(-0.3 * i)
            mix, new_kv = diff_mixer(h, diff_w_qkv[j], diff_q_gain[j], diff_k_gain[j],
                                     diff_lam_q1[j], diff_lam_k1[j], diff_lam_q2[j], diff_lam_k2[j],
                                     diff_sub_gain[j], diff_w_out[j], lam_init, latent, rope, ctx_kv)
        x = x + g1 * mix
        h = rms_norm(x, norm_ffn_g[i]) * (1.0 + sc2) + sh2
        x = x + g2 * moe(h, moe_w_router[i], moe_b_router[i], moe_w_gate_up[i],
                         moe_b_gate_up[i], moe_w_down[i], moe_b_down[i])
        return x, new_kv

    cond_ctx = c_ctx[None, :]
    y = x_prompt
    swa_k, swa_v, diff_k, diff_v = [], [], [], []
    for i in range(DEPTH):
        y, kv = layer(i, y, cond_ctx, False, None, None)
        if i % N_MIXERS == 1:
            swa_k.append(kv[0])
            swa_v.append(kv[1])
        elif i % N_MIXERS == 2:
            diff_k.append(kv[0])
            diff_v.append(kv[1])
    y_prompt = y
    new_swa_k = jnp.stack(swa_k, axis=1)
    new_swa_v = jnp.stack(swa_v, axis=1)
    new_diff_k = jnp.stack(diff_k, axis=1)
    new_diff_v = jnp.stack(diff_v, axis=1)

    rope = axial_rope(x_sample.shape[1])
    z = x_sample
    for i in range(DEPTH):
        j = i // N_MIXERS
        if i % N_MIXERS == 1:
            ctx_kv = (cache_swa_k[:, j], cache_swa_v[:, j])
        elif i % N_MIXERS == 2:
            ctx_kv = (cache_diff_k[:, j], cache_diff_v[:, j])
        else:
            ctx_kv = None
        z, _ = layer(i, z, c, True, rope, ctx_kv)
    y_sample = z

    return (y_prompt, y_sample, new_swa_k, new_swa_v, new_diff_k, new_diff_v)
```

```python
import functools
import math

import jax
import jax.numpy as jnp
from jax import lax
from jax.experimental import pallas as pl
from jax.experimental.pallas import tpu as pltpu

F32 = jnp.float32
BF16 = jnp.bfloat16
I32 = jnp.int32

D = 1024
BATCH = 32
SEQ = 256
DEC_BATCH = 4
DEC_SEQ = 2048
PAST_LEN = 512
DEPTH = 4
GRID_W = 64
HEAD_DIM = 64
SWA_Q_HEADS = 16
SWA_KV_HEADS = 4
SWA_GROUP = SWA_Q_HEADS // SWA_KV_HEADS
SWA_WINDOW = 128
DIFF_HEADS = 8
N_EXPERTS = 32
TOP_K = 4
D_EXPERT = 1024
SWIGLU_ALPHA = 1.702
SWIGLU_LIMIT = 7.0
ROPE_BASE = 10000.0
EPS = 1e-6

T_CTX = BATCH * SEQ
T_LAT = DEC_BATCH * DEC_SEQ
T = T_CTX + T_LAT
N_COND = 1 + DEC_BATCH
COND_PAD = 8

LANES = 128
SUBLANES = 8
BF16_ROWS = 16
MXU_DIM = 256

TM_CONV = 512
TM_QKV = 512
TM_MOE = 256
TM_EXP = 256
CHUNK = BF16_ROWS
N_TILE_MOE = T // TM_MOE
LOCAL_ROWS = TM_MOE * TOP_K + N_EXPERTS * (CHUNK - 1)
LOCAL_ROWS = -(-LOCAL_ROWS // LANES) * LANES
LOCAL_CHUNKS = LOCAL_ROWS // CHUNK
EXP_CHUNKS = TM_EXP // CHUNK
MAX_GROUP_ROWS = T * TOP_K + N_TILE_MOE * N_EXPERTS * (CHUNK - 1)
MAX_EXP_TILES = -(-MAX_GROUP_ROWS // TM_EXP) + N_EXPERTS
MAX_EXP_CHUNKS = MAX_EXP_TILES * EXP_CHUNKS
N_XBUF = 3
assert LOCAL_CHUNKS <= LANES and N_EXPERTS <= LANES

VMEM_LIMIT = 56 * 1024 * 1024


def _cparams(sem, vmem=VMEM_LIMIT):
    return pltpu.CompilerParams(dimension_semantics=sem, vmem_limit_bytes=vmem)


def _dot(a, b):
    return jnp.dot(a, b, preferred_element_type=F32)


def _dot_nt(a, b):
    return lax.dot_general(a, b, (((1,), (1,)), ((), ())), preferred_element_type=F32)


def _dot_tn(a, b):
    return lax.dot_general(a, b, (((0,), (0,)), ((), ())), preferred_element_type=F32)


def _norm_mod(x, g, scale, shift):
    ms = jnp.mean(x * x, axis=-1, keepdims=True)
    y = x * lax.rsqrt(ms + EPS)
    return (y * g) * (1.0 + scale) + shift


MOD_TN = 1536


def _mod_body(cond_ref, w_ref, b_ref, o_ref):
    c = cond_ref[...]
    s = (c * jax.nn.sigmoid(c)).astype(BF16)
    o_ref[0] = _dot(s, w_ref[0].astype(BF16)) + b_ref[0]


def _modulation(cond, ada_w, ada_b):
    n_layer = ada_w.shape[0]
    n_out = ada_w.shape[2]
    return pl.pallas_call(
        _mod_body,
        out_shape=jax.ShapeDtypeStruct((n_layer, COND_PAD, n_out), F32),
        grid=(n_layer, n_out // MOD_TN),
        in_specs=[
            pl.BlockSpec((COND_PAD, D), lambda i, j: (0, 0)),
            pl.BlockSpec((1, D, MOD_TN), lambda i, j: (i, 0, j)),
            pl.BlockSpec((1, 1, MOD_TN), lambda i, j: (i, 0, j)),
        ],
        out_specs=pl.BlockSpec((1, COND_PAD, MOD_TN), lambda i, j: (i, 0, j)),
        compiler_params=_cparams(("arbitrary", "arbitrary")),
        name="modulation",
    )(cond, ada_w, ada_b.reshape(n_layer, 1, n_out))


def _cond_row(row_tile, tile_rows):
    ctx_tiles = T_CTX // tile_rows
    per_batch = DEC_SEQ // tile_rows
    return jnp.where(row_tile < ctx_tiles, 0, 1 + (row_tile - ctx_tiles) // per_batch)


def _conv_body(x_ref, xp_ref, xn_ref, mod_ref, g_ref, win_ref, cw_ref, wout_ref, o_ref):
    t = pl.program_id(0)
    tm = x_ref.shape[0]
    x = x_ref[...]
    m = mod_ref[...]
    shift, scale, gate = m[0:1], m[1:2], m[2:3]
    x_ext = jnp.concatenate([xp_ref[...], x, xn_ref[...]], axis=0)
    h = _norm_mod(x_ext, g_ref[...], scale, shift).astype(BF16)
    proj = _dot(h, win_ref[...])
    gate_b = proj[SUBLANES:SUBLANES + tm, :D]
    u_ext = proj[:, D:2 * D] * proj[:, 2 * D:]
    n_ext = tm + 2 * SUBLANES
    u = u_ext[SUBLANES:SUBLANES + tm]
    u_prev = pltpu.roll(u_ext, 1, axis=0)[SUBLANES:SUBLANES + tm]
    u_next = pltpu.roll(u_ext, n_ext - 1, axis=0)[SUBLANES:SUBLANES + tm]
    row = lax.broadcasted_iota(I32, (tm, 1), 0)
    grow = t * tm + row
    seq_mask = jnp.where(grow < T_CTX, SEQ - 1, DEC_SEQ - 1)
    pos = grow & seq_mask
    u_prev = jnp.where(pos == 0, 0.0, u_prev)
    u_next = jnp.where(pos == seq_mask, 0.0, u_next)
    cw = cw_ref[...]
    conv = u_prev * cw[0:1] + u * cw[1:2] + u_next * cw[2:3]
    y = (gate_b * conv).astype(BF16)
    o_ref[...] = x + gate * _dot(y, wout_ref[...])


def _conv_layer(x, mod_l, norm_g, w_in, conv_w, w_out):
    tm = TM_CONV
    n_tiles = T // tm
    sub_per_tile = tm // SUBLANES
    last_sub = T // SUBLANES - 1
    return pl.pallas_call(
        _conv_body,
        out_shape=jax.ShapeDtypeStruct((T, D), F32),
        grid=(n_tiles,),
        in_specs=[
            pl.BlockSpec((tm, D), lambda t: (t, 0)),
            pl.BlockSpec((SUBLANES, D), lambda t: (jnp.maximum(t * sub_per_tile - 1, 0), 0)),
            pl.BlockSpec((SUBLANES, D), lambda t: (jnp.minimum((t + 1) * sub_per_tile, last_sub), 0)),
            pl.BlockSpec((None, 6, D), lambda t: (_cond_row(t, tm), 0, 0)),
            pl.BlockSpec((1, D), lambda t: (0, 0)),
            pl.BlockSpec((D, 3 * D), lambda t: (0, 0)),
            pl.BlockSpec((3, D), lambda t: (0, 0)),
            pl.BlockSpec((D, D), lambda t: (0, 0)),
        ],
        out_specs=pl.BlockSpec((tm, D), lambda t: (t, 0)),
        compiler_params=_cparams(("arbitrary",)),
        name="conv_mixer",
    )(x, x, x, mod_l, norm_g.reshape(1, D), w_in, conv_w, w_out)


def _head_norm(blk, gain):
    lane = lax.broadcasted_iota(I32, blk.shape, 1)
    low = lane < HEAD_DIM
    s = blk * blk
    s_lo = jnp.sum(jnp.where(low, s, 0.0), axis=-1, keepdims=True)
    s_hi = jnp.sum(jnp.where(low, 0.0, s), axis=-1, keepdims=True)
    ms = jnp.where(low, s_lo, s_hi) * (1.0 / HEAD_DIM)
    return blk * lax.rsqrt(ms + EPS) * gain


def _rope(blk, cos, sin_signed):
    lane = lax.broadcasted_iota(I32, blk.shape, 1)
    first = (lane & 31) < 16
    partner = jnp.where(first, pltpu.roll(blk, LANES - 16, axis=1), pltpu.roll(blk, 16, axis=1))
    return blk * cos + partner * sin_signed


def _qkv_body(*refs, nq, nk, nv, rope, emit_f32, q_scale):
    refs = list(refs)
    x_ref, mod_ref, g_ref, w_ref, qg_ref, kg_ref = refs[:6]
    refs = refs[6:]
    if rope:
        cos_ref, sin_ref = refs[:2]
        refs = refs[2:]
    q_ref, k_ref, v_ref = refs[:3]
    refs = refs[3:]
    if emit_f32:
        kf_ref, vf_ref = refs[:2]
    x = x_ref[...]
    m = mod_ref[...]
    h = _norm_mod(x, g_ref[...], m[1:2], m[0:1]).astype(BF16)
    proj = _dot(h, w_ref[...])
    if rope:
        cos = cos_ref[...]
        sin = sin_ref[...]
    for j in range(nq):
        blk = _head_norm(proj[:, j * LANES:(j + 1) * LANES], qg_ref[...])
        if rope:
            blk = _rope(blk, cos, sin)
        q_ref[:, j * LANES:(j + 1) * LANES] = (blk * q_scale).astype(BF16)
    off = nq * LANES
    for j in range(nk):
        blk = _head_norm(proj[:, off + j * LANES:off + (j + 1) * LANES], kg_ref[...])
        if emit_f32:
            kf_ref[:, j * LANES:(j + 1) * LANES] = blk
        if rope:
            blk = _rope(blk, cos, sin)
        k_ref[:, j * LANES:(j + 1) * LANES] = blk.astype(BF16)
    off = (nq + nk) * LANES
    v = proj[:, off:off + nv]
    v_ref[...] = v.astype(BF16)
    if emit_f32:
        vf_ref[...] = v


def _qkv_proj(x, mod_l, norm_g, w, q_gain, k_gain, rope_tabs, *, latent, nq, nk, nv):
    tm = TM_QKV
    n_rows = T_LAT if latent else T_CTX
    n_tiles = n_rows // tm
    tile0 = (T_CTX // tm) if latent else 0
    n_proj = (nq + nk) * LANES + nv
    gain2 = lambda g: jnp.concatenate([g, g]).reshape(1, LANES)
    in_specs = [
        pl.BlockSpec((tm, D), lambda t: (t + tile0, 0)),
        pl.BlockSpec((None, 6, D), lambda t: (_cond_row(t + tile0, tm), 0, 0)),
        pl.BlockSpec((1, D), lambda t: (0, 0)),
        pl.BlockSpec((D, n_proj), lambda t: (0, 0)),
        pl.BlockSpec((1, LANES), lambda t: (0, 0)),
        pl.BlockSpec((1, LANES), lambda t: (0, 0)),
    ]
    args = [x, mod_l, norm_g.reshape(1, D), w, gain2(q_gain), gain2(k_gain)]
    if latent:
        per_seq = DEC_SEQ // tm
        in_specs += [pl.BlockSpec((tm, LANES), lambda t: (t % per_seq, 0))] * 2
        args += list(rope_tabs)
    out_shape = [
        jax.ShapeDtypeStruct((n_rows, nq * LANES), BF16),
        jax.ShapeDtypeStruct((n_rows, nk * LANES), BF16),
        jax.ShapeDtypeStruct((n_rows, nv), BF16),
    ]
    out_specs = [
        pl.BlockSpec((tm, nq * LANES), lambda t: (t, 0)),
        pl.BlockSpec((tm, nk * LANES), lambda t: (t, 0)),
        pl.BlockSpec((tm, nv), lambda t: (t, 0)),
    ]
    if not latent:
        out_shape += [jax.ShapeDtypeStruct((n_rows, nk * LANES), F32), jax.ShapeDtypeStruct((n_rows, nv), F32)]
        out_specs += [pl.BlockSpec((tm, nk * LANES), lambda t: (t, 0)), pl.BlockSpec((tm, nv), lambda t: (t, 0))]
    body = functools.partial(_qkv_body, nq=nq, nk=nk, nv=nv, rope=latent, emit_f32=not latent,
                             q_scale=HEAD_DIM ** -0.5)
    return pl.pallas_call(
        body,
        out_shape=out_shape,
        grid=(n_tiles,),
        in_specs=in_specs,
        out_specs=out_specs,
        compiler_params=_cparams(("arbitrary",)),
        name="qkv_latent" if latent else "qkv_context",
    )(*args)


def _rope_tables():
    rows = DEC_SEQ // GRID_W
    row_pos = jnp.repeat(jnp.arange(rows, dtype=F32), GRID_W)
    col_pos = jnp.tile(jnp.arange(GRID_W, dtype=F32), rows)
    half = HEAD_DIM // 2
    inv_freq = ROPE_BASE ** (-jnp.arange(0, half, 2, dtype=F32) / half)
    ang_r = row_pos[:, None] * inv_freq[None, :]
    ang_c = col_pos[:, None] * inv_freq[None, :]
    cr, sr, cc, sc = jnp.cos(ang_r), jnp.sin(ang_r), jnp.cos(ang_c), jnp.sin(ang_c)
    cos64 = jnp.concatenate([cr, cr, cc, cc], axis=-1)
    sin64 = jnp.concatenate([-sr, sr, -sc, sc], axis=-1)
    return jnp.concatenate([cos64, cos64], axis=-1), jnp.concatenate([sin64, sin64], axis=-1)


def _swap_halves(blk):
    return pltpu.roll(blk.astype(F32), HEAD_DIM, axis=1).astype(BF16)


def _kv_operands(kh, k_blk, v_blk):
    lane = lax.broadcasted_iota(I32, k_blk.shape, 1)
    low = lane < HEAD_DIM
    k_swp = _swap_halves(k_blk)
    if kh % 2 == 0:
        k2 = jnp.where(low, k_blk, k_swp)
        v_lo = v_blk
    else:
        k2 = jnp.where(low, k_swp, k_blk)
        v_lo = _swap_halves(v_blk)
    v2 = jnp.where(low, v_lo, jnp.ones_like(v_lo))
    return k2, v2


def _stack_query_heads(q_ref, kh, tq):
    lane = lax.broadcasted_iota(I32, (tq, LANES), 1)
    low = lane < HEAD_DIM
    parts = []
    for pair in range(SWA_GROUP // 2):
        blk = (SWA_GROUP // 2) * kh + pair
        qp = q_ref[:, blk * LANES:(blk + 1) * LANES]
        zero = jnp.zeros_like(qp)
        parts += [jnp.where(low, qp, zero), jnp.where(low, zero, qp)]
    return jnp.concatenate(parts, axis=0)


def _sink_column(sink_ref, kh, tq):
    row = lax.broadcasted_iota(I32, (SWA_GROUP * tq, 1), 0)
    col = jnp.full((SWA_GROUP * tq, 1), sink_ref[SWA_GROUP * kh], F32)
    for g in range(1, SWA_GROUP):
        col = jnp.where(row >= g * tq, sink_ref[SWA_GROUP * kh + g], col)
    return col


def _store_head_group(att_ref, kh, tq, acc, sink_term):
    lane = lax.broadcasted_iota(I32, (tq, LANES), 1)
    low = lane < HEAD_DIM
    den = acc[:, HEAD_DIM:HEAD_DIM + 1] + sink_term
    r = acc / den
    r_hi = pltpu.roll(r, HEAD_DIM, axis=1)
    for pair in range(SWA_GROUP // 2):
        blk = (SWA_GROUP // 2) * kh + pair
        a = r[(2 * pair) * tq:(2 * pair + 1) * tq]
        b = r_hi[(2 * pair + 1) * tq:(2 * pair + 2) * tq]
        att_ref[:, blk * LANES:(blk + 1) * LANES] = jnp.where(low, a, b).astype(BF16)


def _swa_ctx_body(sink_ref, x_ref, mod_ref, q_ref, k_ref, v_ref, wout_ref, o_ref, att_ref):
    tq = q_ref.shape[0]
    for kh in range(SWA_KV_HEADS):
        blk = kh // 2
        k2, v2 = _kv_operands(kh, k_ref[:, blk * LANES:(blk + 1) * LANES], v_ref[:, blk * LANES:(blk + 1) * LANES])
        qs = _stack_query_heads(q_ref, kh, tq)
        sink = _sink_column(sink_ref, kh, tq)
        s = _dot_nt(qs, k2)
        mx = jnp.maximum(jnp.max(s, axis=-1, keepdims=True), sink)
        p = jnp.exp(s - mx)
        acc = _dot(p.astype(BF16), v2)
        _store_head_group(att_ref, kh, tq, acc, jnp.exp(sink - mx))
    gate = mod_ref[2:3, :]
    o_ref[...] = x_ref[...] + gate * _dot(att_ref[...], wout_ref[...])


def _swa_ctx_attention(x, mod_l, q, k, v, sink, w_out):
    nkv = SWA_KV_HEADS * HEAD_DIM
    return pl.pallas_call(
        _swa_ctx_body,
        out_shape=jax.ShapeDtypeStruct((T, D), F32),
        grid=(BATCH,),
        in_specs=[
            pl.BlockSpec(memory_space=pltpu.SMEM),
            pl.BlockSpec((SEQ, D), lambda b: (b, 0)),
            pl.BlockSpec((None, 6, D), lambda b: (0, 0, 0)),
            pl.BlockSpec((SEQ, D), lambda b: (b, 0)),
            pl.BlockSpec((SEQ, nkv), lambda b: (b, 0)),
            pl.BlockSpec((SEQ, nkv), lambda b: (b, 0)),
            pl.BlockSpec((D, D), lambda b: (0, 0)),
        ],
        out_specs=pl.BlockSpec((SEQ, D), lambda b: (b, 0)),
        scratch_shapes=[pltpu.VMEM((SEQ, D), BF16)],
        input_output_aliases={1: 0},
        compiler_params=_cparams(("arbitrary",)),
        name="swa_context_attention",
    )(sink, x, mod_l, q, k, v, w_out)


SWA_TQ = 128
SWA_KWIN = 3 * SWA_TQ


def _swa_lat_body(sink_ref, x_ref, mod_ref, q_ref, k_ref, v_ref, kc_ref, vc_ref, wout_ref, o_ref, att_ref):
    j = pl.program_id(1)
    tq = SWA_TQ
    start = jnp.clip((j - 1) * tq, 0, DEC_SEQ - SWA_KWIN)
    start = pl.multiple_of(start, tq)
    row = lax.broadcasted_iota(I32, (SWA_GROUP * tq, 1), 0)
    q_pos = j * tq + (row & (tq - 1))
    k_pos = start + lax.broadcasted_iota(I32, (1, SWA_KWIN), 1)
    valid = jnp.abs(k_pos - q_pos) <= SWA_WINDOW
    for kh in range(SWA_KV_HEADS):
        blk = kh // 2
        lanes = slice(blk * LANES, (blk + 1) * LANES)
        k2, v2 = _kv_operands(kh, k_ref[pl.ds(start, SWA_KWIN), lanes], v_ref[pl.ds(start, SWA_KWIN), lanes])
        kc2, vc2 = _kv_operands(kh, kc_ref[:, lanes], vc_ref[:, lanes])
        qs = _stack_query_heads(q_ref, kh, tq)
        sink = _sink_column(sink_ref, kh, tq)
        s_lat = jnp.where(valid, _dot_nt(qs, k2), -jnp.inf)
        s_ctx = _dot_nt(qs, kc2)
        mx = jnp.maximum(jnp.max(s_lat, axis=-1, keepdims=True), jnp.max(s_ctx, axis=-1, keepdims=True))
        mx = jnp.maximum(mx, sink)
        p_lat = jnp.exp(s_lat - mx).astype(BF16)
        p_ctx = jnp.exp(s_ctx - mx).astype(BF16)
        acc = _dot(p_lat, v2) + _dot(p_ctx, vc2)
        _store_head_group(att_ref, kh, tq, acc, jnp.exp(sink - mx))
    gate = mod_ref[2:3, :]
    o_ref[...] = x_ref[...] + gate * _dot(att_ref[...], wout_ref[...])


def _swa_lat_attention(x, mod_l, q, k, v, k_cache, v_cache, sink, w_out):
    nkv = SWA_KV_HEADS * HEAD_DIM
    nb = DEC_SEQ // SWA_TQ
    x_tile0 = T_CTX // SWA_TQ
    return pl.pallas_call(
        _swa_lat_body,
        out_shape=jax.ShapeDtypeStruct((T, D), F32),
        grid=(DEC_BATCH, nb),
        in_specs=[
            pl.BlockSpec(memory_space=pltpu.SMEM),
            pl.BlockSpec((SWA_TQ, D), lambda b, j: (x_tile0 + b * nb + j, 0)),
            pl.BlockSpec((None, 6, D), lambda b, j: (1 + b, 0, 0)),
            pl.BlockSpec((SWA_TQ, D), lambda b, j: (b * nb + j, 0)),
            pl.BlockSpec((DEC_SEQ, nkv), lambda b, j: (b, 0)),
            pl.BlockSpec((DEC_SEQ, nkv), lambda b, j: (b, 0)),
            pl.BlockSpec((None, PAST_LEN, nkv), lambda b, j: (b, 0, 0)),
            pl.BlockSpec((None, PAST_LEN, nkv), lambda b, j: (b, 0, 0)),
            pl.BlockSpec((D, D), lambda b, j: (0, 0)),
        ],
        out_specs=pl.BlockSpec((SWA_TQ, D), lambda b, j: (x_tile0 + b * nb + j, 0)),
        scratch_shapes=[pltpu.VMEM((SWA_TQ, D), BF16)],
        input_output_aliases={1: 0},
        compiler_params=_cparams(("arbitrary", "arbitrary")),
        name="swa_latent_attention",
    )(sink, x, mod_l, q, k, v, k_cache, v_cache, w_out)


DIFF_TQ = 256


def _diff_body(x_ref, mod_ref, q_ref, k_ref, v_ref, lam_ref, sg_ref, wout_ref, o_ref, att_ref, *, lam_init):
    tq = q_ref.shape[0]
    lane = lax.broadcasted_iota(I32, (tq, LANES), 1)
    low = lane < HEAD_DIM
    lv = lam_ref[...]
    lam = (jnp.exp(jnp.sum(lv[0:1] * lv[1:2], axis=-1, keepdims=True))
           - jnp.exp(jnp.sum(lv[2:3] * lv[3:4], axis=-1, keepdims=True)) + lam_init)
    sub_gain = sg_ref[...]
    for hd in range(DIFF_HEADS):
        lanes = slice(hd * LANES, (hd + 1) * LANES)
        qh = q_ref[:, lanes]
        vh = v_ref[:, lanes]
        zero = jnp.zeros_like(qh)
        q2 = jnp.concatenate([jnp.where(low, qh, zero), jnp.where(low, zero, qh)], axis=0)
        s = _dot_nt(q2, k_ref[:, lanes])
        p = jnp.exp(s - jnp.max(s, axis=-1, keepdims=True)).astype(BF16)
        acc = _dot(p, jnp.concatenate([vh, jnp.ones_like(vh)], axis=1))
        r = acc[:, :LANES] / acc[:, LANES:LANES + 1]
        o = r[:tq] - lam * r[tq:]
        ms = jnp.mean(o * o, axis=-1, keepdims=True)
        o = (o * lax.rsqrt(ms + EPS) * sub_gain) * (1.0 - lam_init)
        att_ref[:, lanes] = o.astype(BF16)
    gate = mod_ref[2:3, :]
    o_ref[...] = x_ref[...] + gate * _dot(att_ref[...], wout_ref[...])


def _diff_attention(x, mod_l, q, k_all, v_all, lam_vecs, sub_gain, w_out, *, latent, lam_init):
    tq = DIFF_TQ
    n_keys = k_all.shape[1]
    if latent:
        per_batch = DEC_SEQ // tq
        grid = (DEC_BATCH, per_batch)
        x_tile0 = T_CTX // tq
        x_map = lambda b, j: (x_tile0 + b * per_batch + j, 0)
        q_map = lambda b, j: (b * per_batch + j, 0)
        mod_map = lambda b, j: (1 + b, 0, 0)
    else:
        per_batch = SEQ // tq
        grid = (BATCH, per_batch)
        x_map = lambda b, j: (b * per_batch + j, 0)
        q_map = x_map
        mod_map = lambda b, j: (0, 0, 0)
    const2 = lambda b, j: (0, 0)
    return pl.pallas_call(
        functools.partial(_diff_body, lam_init=lam_init),
        out_shape=jax.ShapeDtypeStruct((T, D), F32),
        grid=grid,
        in_specs=[
            pl.BlockSpec((tq, D), x_map),
            pl.BlockSpec((None, 6, D), mod_map),
            pl.BlockSpec((tq, D), q_map),
            pl.BlockSpec((None, n_keys, D), lambda b, j: (b, 0, 0)),
            pl.BlockSpec((None, n_keys, D), lambda b, j: (b, 0, 0)),
            pl.BlockSpec((4, HEAD_DIM), const2),
            pl.BlockSpec((1, 2 * HEAD_DIM), const2),
            pl.BlockSpec((D, D), const2),
        ],
        out_specs=pl.BlockSpec((tq, D), x_map),
        scratch_shapes=[pltpu.VMEM((tq, D), BF16)],
        input_output_aliases={0: 0},
        compiler_params=_cparams(("arbitrary", "arbitrary")),
        name="diff_latent_attention" if latent else "diff_context_attention",
    )(x, mod_l, q, k_all, v_all, lam_vecs, sub_gain, w_out)


def _dispatch_body(x_ref, mod_ref, g_ref, wr_hi_ref, wr_lo_ref, br_ref, ltri_ref, utri_ref, incl_ref,
                   z_ref, pg_ref, meta_ref, seen_ref):
    tm = TM_MOE
    t = pl.program_id(0)

    @pl.when(t == 0)
    def _():
        seen_ref[...] = jnp.zeros_like(seen_ref)

    m = mod_ref[...]
    h = _norm_mod(x_ref[...], g_ref[...], m[4:5], m[3:4])
    hb = h.astype(BF16)
    h_lo = (h - hb.astype(F32)).astype(BF16)
    wr_hi = wr_hi_ref[...]
    logits = _dot(hb, wr_hi) + _dot(h_lo, wr_hi) + _dot(hb, wr_lo_ref[...]) + br_ref[...]
    lane = lax.broadcasted_iota(I32, (tm, LANES), 1)
    work = jnp.where(lane < N_EXPERTS, logits, -jnp.inf)
    picks, vals = [], []
    for _ in range(TOP_K):
        best = jnp.max(work, axis=-1, keepdims=True)
        idx = jnp.min(jnp.where(work == best, lane, LANES), axis=-1, keepdims=True)
        hit = lane == idx
        work = jnp.where(hit, -jnp.inf, work)
        picks.append(hit)
        vals.append(best)
    exps = [jnp.exp(v - vals[0]) for v in vals]
    den = exps[0] + exps[1] + exps[2] + exps[3]
    gates = [e / den for e in exps]
    chosen = jnp.where(picks[0] | picks[1] | picks[2] | picks[3], 1.0, 0.0)
    chosen_b = chosen.astype(BF16)

    rank = _dot(ltri_ref[...], chosen_b)
    count = jnp.sum(chosen, axis=0, keepdims=True)
    chunks = jnp.floor((count + (CHUNK - 1)) * (1.0 / CHUNK))
    chunks8 = jnp.broadcast_to(chunks, (SUBLANES, LANES))
    start = _dot(chunks8.astype(BF16), utri_ref[...])
    pos_full = start[0:1] * CHUNK + rank

    col = lax.broadcasted_iota(I32, (tm, LOCAL_ROWS), 1).astype(F32)
    sel = jnp.zeros((tm, LOCAL_ROWS), F32)
    pg = jnp.zeros((tm, LANES), F32)
    for k in range(TOP_K):
        pos_k = jnp.sum(jnp.where(picks[k], pos_full, 0.0), axis=-1, keepdims=True)
        sel = jnp.where(col == pos_k, 1.0, sel)
        pg = jnp.where(lane == k, pos_k, pg)
        pg = jnp.where(lane == TOP_K + k, gates[k], pg)
    rows = _dot_tn(sel.astype(BF16), hb)
    z_ref[...] = rows.astype(BF16).reshape(LOCAL_CHUNKS, CHUNK, D)
    pg_ref[...] = pg

    ones = jnp.ones((tm, LANES), BF16)
    count_c = _dot_tn(chosen_b, ones)
    chunks_c = jnp.floor((count_c + (CHUNK - 1)) * (1.0 / CHUNK))
    end_c = _dot(incl_ref[...], chunks_c.astype(BF16))
    first_c = end_c - chunks_c
    j_lane = lax.broadcasted_iota(I32, (LANES, LANES), 1).astype(F32)
    e_sub = lax.broadcasted_iota(I32, (LANES, LANES), 0).astype(F32)
    e_of_j = jnp.sum(jnp.where(end_c <= j_lane, 1.0, 0.0), axis=0, keepdims=True)
    used = j_lane[0:1] < end_c[LANES - 1:LANES]
    delta = jnp.sum(jnp.where(e_sub == e_of_j, seen_ref[...] - first_c, 0.0), axis=0, keepdims=True)
    rel = jnp.where(used, j_lane[0:1] + delta, 0.0)
    e_of_j = jnp.where(used, e_of_j, 0.0)
    seen_ref[...] += chunks_c

    sub = lax.broadcasted_iota(I32, (SUBLANES, LANES), 0)
    meta = jnp.where(sub == 0, chunks8, 0.0)
    meta = jnp.where(sub == 1, start, meta)
    meta = jnp.where(sub == 2, rel, meta)
    meta = jnp.where(sub == 3, e_of_j, meta)
    meta_ref[0] = meta.astype(I32)


def _moe_dispatch(x, mod_l, norm_g, w_router, b_router):
    wr = jnp.pad(w_router, ((0, 0), (0, LANES - N_EXPERTS)))
    wr_hi = wr.astype(BF16)
    wr_lo = (wr - wr_hi.astype(F32)).astype(BF16)
    br = jnp.pad(b_router, (0, LANES - N_EXPERTS)).reshape(1, LANES)
    r = jnp.arange(TM_MOE)
    ltri = (r[None, :] < r[:, None]).astype(BF16)
    c = jnp.arange(LANES)
    utri = (c[:, None] < c[None, :]).astype(BF16)
    incl = (c[None, :] <= c[:, None]).astype(BF16)
    const2 = lambda t: (0, 0)
    return pl.pallas_call(
        _dispatch_body,
        out_shape=[
            jax.ShapeDtypeStruct((N_TILE_MOE * LOCAL_CHUNKS, CHUNK, D), BF16),
            jax.ShapeDtypeStruct((T, LANES), F32),
            jax.ShapeDtypeStruct((N_TILE_MOE, SUBLANES, LANES), I32),
        ],
        grid=(N_TILE_MOE,),
        in_specs=[
            pl.BlockSpec((TM_MOE, D), lambda t: (t, 0)),
            pl.BlockSpec((None, 6, D), lambda t: (_cond_row(t, TM_MOE), 0, 0)),
            pl.BlockSpec((1, D), const2),
            pl.BlockSpec((D, LANES), const2),
            pl.BlockSpec((D, LANES), const2),
            pl.BlockSpec((1, LANES), const2),
            pl.BlockSpec((TM_MOE, TM_MOE), const2),
            pl.BlockSpec((LANES, LANES), const2),
            pl.BlockSpec((LANES, LANES), const2),
        ],
        out_specs=[
            pl.BlockSpec((LOCAL_CHUNKS, CHUNK, D), lambda t: (t, 0, 0)),
            pl.BlockSpec((TM_MOE, LANES), lambda t: (t, 0)),
            pl.BlockSpec((1, SUBLANES, LANES), lambda t: (t, 0, 0)),
        ],
        scratch_shapes=[pltpu.VMEM((LANES, LANES), F32)],
        compiler_params=_cparams(("arbitrary",)),
        name="moe_dispatch",
    )(x, mod_l, norm_g.reshape(1, D), wr_hi, wr_lo, br, ltri, utri, incl)


ST_FILLED, ST_EXPERT, ST_TILE, ST_CHUNK, ST_LEFT = range(5)


def _expert_body(cnt_ref, first_ref, tstart_ref, ntile_ref, z_hbm, wgu_ref, bgu_ref, wd_ref, bd_ref, perm_ref,
                 y_hbm, wgu_s, wd_s, xbuf, ybuf, st, sem_in, sem_out):
    e = pl.program_id(0)
    t0 = tstart_ref[e]
    nt = ntile_ref[e]
    total = tstart_ref[N_EXPERTS - 1] + ntile_ref[N_EXPERTS - 1]

    def fill_next_tile():
        filled = st[ST_FILLED]
        slot = lax.rem(filled, N_XBUF)
        prev_e = st[ST_EXPERT]
        e_g, left = lax.while_loop(
            lambda v: v[1] == 0,
            lambda v: (v[0] + 1, ntile_ref[jnp.minimum(v[0] + 1, N_EXPERTS - 1)]),
            (prev_e, st[ST_LEFT]))
        fresh = e_g != prev_e
        tile = jnp.where(fresh, 0, st[ST_TILE])
        chunk = jnp.where(fresh, 0, st[ST_CHUNK])

        def one_chunk(k, carry):
            tile, chunk = lax.while_loop(
                lambda v: (v[0] < N_TILE_MOE)
                & (v[1] >= cnt_ref[jnp.minimum(v[0], N_TILE_MOE - 1) * N_EXPERTS + e_g]),
                lambda v: (v[0] + 1, 0),
                carry)
            live = tile < N_TILE_MOE
            tl = jnp.minimum(tile, N_TILE_MOE - 1)
            src = jnp.where(live, tl * LOCAL_CHUNKS + first_ref[tl * N_EXPERTS + e_g] + chunk, 0)
            pltpu.make_async_copy(z_hbm.at[src], xbuf.at[slot, k], sem_in.at[slot]).start()
            return tile, jnp.where(live, chunk + 1, chunk)

        tile, chunk = lax.fori_loop(0, EXP_CHUNKS, one_chunk, (tile, chunk))
        st[ST_FILLED] = filled + 1
        st[ST_EXPERT] = e_g
        st[ST_TILE] = tile
        st[ST_CHUNK] = chunk
        st[ST_LEFT] = left - 1

    def fill_if_any():
        @pl.when(st[ST_FILLED] < total)
        def _():
            fill_next_tile()

    def wait_gather(slot):
        pltpu.make_async_copy(z_hbm.at[pl.ds(0, EXP_CHUNKS)], xbuf.at[slot], sem_in.at[slot]).wait()

    def out_copy(g, slot):
        return pltpu.make_async_copy(ybuf.at[slot], y_hbm.at[pl.ds(g * EXP_CHUNKS, EXP_CHUNKS)],
                                     sem_out.at[slot])

    @pl.when(e == 0)
    def _():
        st[ST_FILLED] = 0
        st[ST_EXPERT] = -1
        st[ST_TILE] = 0
        st[ST_CHUNK] = 0
        st[ST_LEFT] = 0
        for _ in range(N_XBUF - 1):
            fill_if_any()

    @pl.when(nt > 0)
    def _():
        for blk in range(2 * D_EXPERT // MXU_DIM):
            cols = slice(blk * MXU_DIM, (blk + 1) * MXU_DIM)
            wgu_s[:, cols] = _dot(wgu_ref[:, cols].astype(BF16), perm_ref[...]).astype(BF16)
        wd_s[...] = wd_ref[...].astype(BF16)

    def tile_body(i, carry):
        g = t0 + i
        slot = lax.rem(g, N_XBUF)
        oslot = lax.rem(g, 2)
        fill_if_any()
        wait_gather(slot)
        x = xbuf[slot].reshape(TM_EXP, D)
        acts = []
        for blk in range(2 * D_EXPERT // MXU_DIM):
            cols = slice(blk * MXU_DIM, (blk + 1) * MXU_DIM)
            gu = _dot(x, wgu_s[:, cols]) + bgu_ref[:, cols]
            x_glu = jnp.minimum(gu[:, :LANES], SWIGLU_LIMIT)
            x_lin = jnp.clip(gu[:, LANES:], -SWIGLU_LIMIT, SWIGLU_LIMIT)
            acts.append((x_glu * jax.nn.sigmoid(SWIGLU_ALPHA * x_glu) * (x_lin + 1.0)).astype(BF16))
        y = _dot(jnp.concatenate(acts, axis=1), wd_s[...]) + bd_ref[...]

        @pl.when(g >= 2)
        def _():
            out_copy(g - 2, oslot).wait()

        ybuf[oslot] = y.astype(BF16).reshape(EXP_CHUNKS, CHUNK, D)
        out_copy(g, oslot).start()
        return carry

    lax.fori_loop(0, nt, tile_body, 0)

    @pl.when(e == N_EXPERTS - 1)
    def _():
        last = total - 1
        out_copy(last, lax.rem(last, 2)).wait()

        @pl.when(total >= 2)
        def _():
            out_copy(last - 1, lax.rem(last - 1, 2)).wait()

        ybuf[0] = jnp.zeros((EXP_CHUNKS, CHUNK, D), BF16)

        def start_fill(g, carry):
            out_copy(g, 0).start()
            return carry

        def wait_fill(g, carry):
            out_copy(g, 0).wait()
            return carry

        lax.fori_loop(total, MAX_EXP_TILES, start_fill, 0)
        lax.fori_loop(total, MAX_EXP_TILES, wait_fill, 0)


def _moe_experts(z, cnt, first, tstart, ntile, layer, w_gate_up, bgu_all, w_down, b_down):
    n_gu = 2 * D_EXPERT
    i = jnp.arange(MXU_DIM)
    perm = (i[:, None] == jnp.where(i < LANES, 2 * i, 2 * (i - LANES) + 1)[None, :]).astype(BF16)
    wmap = lambda e, *_: (layer, e, 0, 0)
    grid_spec = pltpu.PrefetchScalarGridSpec(
        num_scalar_prefetch=4,
        grid=(N_EXPERTS,),
        in_specs=[
            pl.BlockSpec(memory_space=pl.ANY),
            pl.BlockSpec((None, None, D, n_gu), wmap),
            pl.BlockSpec((None, None, 1, n_gu), wmap),
            pl.BlockSpec((None, None, D_EXPERT, D), wmap),
            pl.BlockSpec((None, None, 1, D), wmap),
            pl.BlockSpec((MXU_DIM, MXU_DIM), lambda e, *_: (0, 0)),
        ],
        out_specs=pl.BlockSpec(memory_space=pl.ANY),
        scratch_shapes=[
            pltpu.VMEM((D, n_gu), BF16),
            pltpu.VMEM((D_EXPERT, D), BF16),
            pltpu.VMEM((N_XBUF, EXP_CHUNKS, CHUNK, D), BF16),
            pltpu.VMEM((2, EXP_CHUNKS, CHUNK, D), BF16),
            pltpu.SMEM((SUBLANES,), I32),
            pltpu.SemaphoreType.DMA((N_XBUF,)),
            pltpu.SemaphoreType.DMA((2,)),
        ],
    )
    return pl.pallas_call(
        _expert_body,
        out_shape=jax.ShapeDtypeStruct((MAX_EXP_CHUNKS, CHUNK, D), BF16),
        grid_spec=grid_spec,
        compiler_params=_cparams(("arbitrary",)),
        name="moe_experts",
    )(cnt, first, tstart, ntile, z, w_gate_up, bgu_all, w_down,
      b_down.reshape(DEPTH, N_EXPERTS, 1, D), perm)


def _combine_body(rel_ref, eidx_ref, base_ref, x_ref, mod_ref, pg_ref, y_hbm, o_ref, ybuf, sem):
    t = pl.program_id(0)
    n_t = pl.num_programs(0)
    slot = lax.rem(t, 2)

    def start_gather(tile, sl):
        for c in range(LOCAL_CHUNKS):
            k = tile * LANES + c
            src = rel_ref[k] + base_ref[eidx_ref[k]]
            pltpu.make_async_copy(y_hbm.at[src], ybuf.at[sl, c], sem.at[sl]).start()

    @pl.when(t == 0)
    def _():
        start_gather(0, 0)

    @pl.when(t + 1 < n_t)
    def _():
        start_gather(t + 1, 1 - slot)

    pltpu.make_async_copy(y_hbm.at[pl.ds(0, LOCAL_CHUNKS)], ybuf.at[slot], sem.at[slot]).wait()
    y = ybuf[slot].reshape(LOCAL_ROWS, D)
    pg = pg_ref[...]
    col = lax.broadcasted_iota(I32, (TM_MOE, LOCAL_ROWS), 1).astype(F32)
    comb = jnp.zeros((TM_MOE, LOCAL_ROWS), F32)
    for k in range(TOP_K):
        comb = jnp.where(col == pg[:, k:k + 1], pg[:, TOP_K + k:TOP_K + k + 1], comb)
    moe = _dot(comb.astype(BF16), y)
    o_ref[...] = x_ref[...] + mod_ref[5:6, :] * moe


def _moe_combine(x, mod_l, pg, y, rel, eidx, base):
    grid_spec = pltpu.PrefetchScalarGridSpec(
        num_scalar_prefetch=3,
        grid=(N_TILE_MOE,),
        in_specs=[
            pl.BlockSpec((TM_MOE, D), lambda t, *_: (t, 0)),
            pl.BlockSpec((None, 6, D), lambda t, *_: (_cond_row(t, TM_MOE), 0, 0)),
            pl.BlockSpec((TM_MOE, LANES), lambda t, *_: (t, 0)),
            pl.BlockSpec(memory_space=pl.ANY),
        ],
        out_specs=pl.BlockSpec((TM_MOE, D), lambda t, *_: (t, 0)),
        scratch_shapes=[
            pltpu.VMEM((2, LOCAL_CHUNKS, CHUNK, D), BF16),
            pltpu.SemaphoreType.DMA((2,)),
        ],
    )
    return pl.pallas_call(
        _combine_body,
        out_shape=jax.ShapeDtypeStruct((T, D), F32),
        grid_spec=grid_spec,
        input_output_aliases={3: 0},
        compiler_params=_cparams(("arbitrary",)),
        name="moe_combine",
    )(rel, eidx, base, x, mod_l, pg, y)


def _moe_layer(x, mod_l, norm_g, layer, w_router, b_router, w_gate_up, bgu_all, w_down, b_down):
    z, pg, meta = _moe_dispatch(x, mod_l, norm_g, w_router, b_router)
    cnt = meta[:, 0, :N_EXPERTS]
    first = meta[:, 1, :N_EXPERTS]
    ntile = (jnp.sum(cnt, axis=0) + EXP_CHUNKS - 1) // EXP_CHUNKS
    tstart = jnp.cumsum(ntile) - ntile
    y = _moe_experts(z, cnt.reshape(-1), first.reshape(-1), tstart, ntile, layer,
                     w_gate_up, bgu_all, w_down, b_down)
    return _moe_combine(x, mod_l, pg, y, meta[:, 2, :].reshape(-1), meta[:, 3, :].reshape(-1),
                        tstart * EXP_CHUNKS)


def kernel(x_prompt, x_sample, c, cache_swa_k, cache_swa_v, cache_diff_k, cache_diff_v, c_ctx, norm_mix_g, norm_ffn_g, ada_w, ada_b, conv_w_in, conv_w, conv_w_out, swa_w_qkv, swa_q_gain, swa_k_gain, swa_sink, swa_w_out, diff_w_qkv, diff_q_gain, diff_k_gain, diff_lam_q1, diff_lam_k1, diff_lam_q2, diff_lam_k2, diff_sub_gain, diff_w_out, moe_w_router, moe_b_router, moe_w_gate_up, moe_b_gate_up, moe_w_down, moe_b_down):
    x = jnp.concatenate([x_prompt.reshape(T_CTX, D), x_sample.reshape(T_LAT, D)], axis=0)
    cond = jnp.concatenate([c_ctx[None, :], c, jnp.zeros((COND_PAD - N_COND, D), F32)], axis=0)
    mod = _modulation(cond, ada_w, ada_b).reshape(DEPTH, COND_PAD, 6, D)
    rope_tabs = _rope_tables()
    n_gu = 2 * D_EXPERT
    bgu_all = moe_b_gate_up.reshape(DEPTH, N_EXPERTS, n_gu // MXU_DIM, LANES, 2)
    bgu_all = bgu_all.transpose(0, 1, 2, 4, 3).reshape(DEPTH, N_EXPERTS, 1, n_gu)
    new_kv = {}
    for i in range(DEPTH):
        j = i // 3
        kind = i % 3
        mod_l = mod[i]
        if kind == 0:
            x = _conv_layer(x, mod_l, norm_mix_g[i], conv_w_in[j].astype(BF16), conv_w[j],
                            conv_w_out[j].astype(BF16))
        elif kind == 1:
            w = swa_w_qkv[j].astype(BF16)
            w_out = swa_w_out[j].astype(BF16)
            nkv = SWA_KV_HEADS * HEAD_DIM
            dims = dict(nq=SWA_Q_HEADS * HEAD_DIM // LANES, nk=nkv // LANES, nv=nkv)
            q, k, v, kf, vf = _qkv_proj(x, mod_l, norm_mix_g[i], w, swa_q_gain[j], swa_k_gain[j], None,
                                        latent=False, **dims)
            new_kv["swa_k"] = kf.reshape(BATCH, 1, SEQ, SWA_KV_HEADS, HEAD_DIM)
            new_kv["swa_v"] = vf.reshape(BATCH, 1, SEQ, SWA_KV_HEADS, HEAD_DIM)
            x = _swa_ctx_attention(x, mod_l, q, k, v, swa_sink[j], w_out)
            q, k, v = _qkv_proj(x, mod_l, norm_mix_g[i], w, swa_q_gain[j], swa_k_gain[j], rope_tabs,
                                latent=True, **dims)
            kc = cache_swa_k[:, j].reshape(DEC_BATCH, PAST_LEN, nkv).astype(BF16)
            vc = cache_swa_v[:, j].reshape(DEC_BATCH, PAST_LEN, nkv).astype(BF16)
            x = _swa_lat_attention(x, mod_l, q, k, v, kc, vc, swa_sink[j], w_out)
        else:
            w = diff_w_qkv[j].astype(BF16)
            w_out = diff_w_out[j].astype(BF16)
            lam_init = 0.8 - 0.6 * math.exp(-0.3 * i)
            lam_vecs = jnp.stack([diff_lam_q1[j], diff_lam_k1[j], diff_lam_q2[j], diff_lam_k2[j]])
            sub_gain = diff_sub_gain[j].reshape(1, 2 * HEAD_DIM)
            dims = dict(nq=D // LANES, nk=D // LANES, nv=D)
            q, k, v, kf, vf = _qkv_proj(x, mod_l, norm_mix_g[i], w, diff_q_gain[j], diff_k_gain[j], None,
                                        latent=False, **dims)
            new_kv["diff_k"] = kf.reshape(BATCH, 1, SEQ, DIFF_HEADS, 2, HEAD_DIM)
            new_kv["diff_v"] = vf.reshape(BATCH, 1, SEQ, DIFF_HEADS, 2 * HEAD_DIM)
            x = _diff_attention(x, mod_l, q, k.reshape(BATCH, SEQ, D), v.reshape(BATCH, SEQ, D), lam_vecs,
                                sub_gain, w_out, latent=False, lam_init=lam_init)
            q, k, v = _qkv_proj(x, mod_l, norm_mix_g[i], w, diff_q_gain[j], diff_k_gain[j], rope_tabs,
                                latent=True, **dims)
            kc = cache_diff_k[:, j].reshape(DEC_BATCH, PAST_LEN, D).astype(BF16)
            vc = cache_diff_v[:, j].reshape(DEC_BATCH, PAST_LEN, D).astype(BF16)
            k_all = jnp.concatenate([k.reshape(DEC_BATCH, DEC_SEQ, D), kc], axis=1)
            v_all = jnp.concatenate([v.reshape(DEC_BATCH, DEC_SEQ, D), vc], axis=1)
            x = _diff_attention(x, mod_l, q, k_all, v_all, lam_vecs, sub_gain, w_out, latent=True,
                                lam_init=lam_init)
        x = _moe_layer(x, mod_l, norm_ffn_g[i], i, moe_w_router[i], moe_b_router[i], moe_w_gate_up,
                       bgu_all, moe_w_down, moe_b_down)
    y_prompt = x[:T_CTX].reshape(BATCH, SEQ, D)
    y_sample = x[T_CTX:].reshape(DEC_BATCH, DEC_SEQ, D)
    return (y_prompt, y_sample, new_kv["swa_k"], new_kv["swa_v"], new_kv["diff_k"], new_kv["diff_v"])
```

```python
import functools
import math

import jax
import jax.numpy as jnp
from jax import lax
from jax.experimental import pallas as pl
from jax.experimental.pallas import tpu as pltpu

F32 = jnp.float32
BF16 = jnp.bfloat16
I32 = jnp.int32

D = 1024
BATCH = 32
SEQ = 256
DEC_BATCH = 4
DEC_SEQ = 2048
PAST_LEN = 512
DEPTH = 4
GRID_W = 64
HEAD_DIM = 64
SWA_Q_HEADS = 16
SWA_KV_HEADS = 4
SWA_GROUP = SWA_Q_HEADS // SWA_KV_HEADS
SWA_WINDOW = 128
DIFF_HEADS = 8
N_EXPERTS = 32
TOP_K = 4
D_EXPERT = 1024
SWIGLU_ALPHA = 1.702
SWIGLU_LIMIT = 7.0
ROPE_BASE = 10000.0
EPS = 1e-6

T_CTX = BATCH * SEQ
T_LAT = DEC_BATCH * DEC_SEQ
T = T_CTX + T_LAT
N_COND = 1 + DEC_BATCH
COND_PAD = 8

LANES = 128
SUBLANES = 8
BF16_ROWS = 16
MXU_DIM = 256

TM_CONV = 512
TM_QKV = 512
TM_MOE = 256
TM_EXP = 256
CHUNK = SUBLANES
HALF = D // 2
assert 2 * CHUNK == BF16_ROWS
N_TILE_MOE = T // TM_MOE
LOCAL_ROWS = TM_MOE * TOP_K + N_EXPERTS * (CHUNK - 1)
LOCAL_ROWS = -(-LOCAL_ROWS // LANES) * LANES
LOCAL_CHUNKS = LOCAL_ROWS // CHUNK
EXP_CHUNKS = TM_EXP // CHUNK
MAX_GROUP_ROWS = T * TOP_K + N_TILE_MOE * N_EXPERTS * (CHUNK - 1)
MAX_EXP_TILES = -(-MAX_GROUP_ROWS // TM_EXP) + N_EXPERTS
MAX_EXP_CHUNKS = MAX_EXP_TILES * EXP_CHUNKS
N_XBUF = 3


def _chunks_to_storage(rows):
    r3 = rows.reshape(rows.shape[0] // CHUNK, CHUNK, D)
    return jnp.concatenate([r3[:, :, :HALF], r3[:, :, HALF:]], axis=1).astype(BF16)


def _chunks_from_storage(stored):
    s = stored.astype(F32)
    r3 = jnp.concatenate([s[:, :CHUNK, :], s[:, CHUNK:, :]], axis=-1)
    return r3.reshape(stored.shape[0] * CHUNK, D).astype(BF16)

VMEM_LIMIT = 56 * 1024 * 1024


def _cparams(sem, vmem=VMEM_LIMIT):
    return pltpu.CompilerParams(dimension_semantics=sem, vmem_limit_bytes=vmem)


def _dot(a, b):
    return jnp.dot(a, b, preferred_element_type=F32)


def _dot_nt(a, b):
    return lax.dot_general(a, b, (((1,), (1,)), ((), ())), preferred_element_type=F32)


def _dot_tn(a, b):
    return lax.dot_general(a, b, (((0,), (0,)), ((), ())), preferred_element_type=F32)


def _norm_mod(x, g, scale, shift):
    ms = jnp.mean(x * x, axis=-1, keepdims=True)
    y = x * lax.rsqrt(ms + EPS)
    return (y * g) * (1.0 + scale) + shift


MOD_TN = 1536


def _mod_body(cond_ref, w_ref, b_ref, o_ref):
    c = cond_ref[...]
    s = (c * jax.nn.sigmoid(c)).astype(BF16)
    o_ref[0] = _dot(s, w_ref[0].astype(BF16)) + b_ref[0]


def _modulation(cond, ada_w, ada_b):
    n_layer = ada_w.shape[0]
    n_out = ada_w.shape[2]
    return pl.pallas_call(
        _mod_body,
        out_shape=jax.ShapeDtypeStruct((n_layer, COND_PAD, n_out), F32),
        grid=(n_layer, n_out // MOD_TN),
        in_specs=[
            pl.BlockSpec((COND_PAD, D), lambda i, j: (0, 0)),
            pl.BlockSpec((1, D, MOD_TN), lambda i, j: (i, 0, j)),
            pl.BlockSpec((1, 1, MOD_TN), lambda i, j: (i, 0, j)),
        ],
        out_specs=pl.BlockSpec((1, COND_PAD, MOD_TN), lambda i, j: (i, 0, j)),
        compiler_params=_cparams(("arbitrary", "arbitrary")),
        name="modulation",
    )(cond, ada_w, ada_b.reshape(n_layer, 1, n_out))


def _cond_row(row_tile, tile_rows):
    ctx_tiles = T_CTX // tile_rows
    per_batch = DEC_SEQ // tile_rows
    return jnp.where(row_tile < ctx_tiles, 0, 1 + (row_tile - ctx_tiles) // per_batch)


def _conv_body(x_ref, xp_ref, xn_ref, mod_ref, g_ref, win_ref, cw_ref, wout_ref, o_ref):
    t = pl.program_id(0)
    tm = x_ref.shape[0]
    x = x_ref[...]
    m = mod_ref[...]
    shift, scale, gate = m[0:1], m[1:2], m[2:3]
    x_ext = jnp.concatenate([xp_ref[...], x, xn_ref[...]], axis=0)
    h = _norm_mod(x_ext, g_ref[...], scale, shift).astype(BF16)
    proj = _dot(h, win_ref[...])
    gate_b = proj[SUBLANES:SUBLANES + tm, :D]
    u_ext = proj[:, D:2 * D] * proj[:, 2 * D:]
    n_ext = tm + 2 * SUBLANES
    u = u_ext[SUBLANES:SUBLANES + tm]
    u_prev = pltpu.roll(u_ext, 1, axis=0)[SUBLANES:SUBLANES + tm]
    u_next = pltpu.roll(u_ext, n_ext - 1, axis=0)[SUBLANES:SUBLANES + tm]
    row = lax.broadcasted_iota(I32, (tm, 1), 0)
    grow = t * tm + row
    seq_mask = jnp.where(grow < T_CTX, SEQ - 1, DEC_SEQ - 1)
    pos = grow & seq_mask
    u_prev = jnp.where(pos == 0, 0.0, u_prev)
    u_next = jnp.where(pos == seq_mask, 0.0, u_next)
    cw = cw_ref[...]
    conv = u_prev * cw[0:1] + u * cw[1:2] + u_next * cw[2:3]
    y = (gate_b * conv).astype(BF16)
    o_ref[...] = x + gate * _dot(y, wout_ref[...])


def _conv_layer(x, mod_l, norm_g, w_in, conv_w, w_out):
    tm = TM_CONV
    n_tiles = T // tm
    sub_per_tile = tm // SUBLANES
    last_sub = T // SUBLANES - 1
    return pl.pallas_call(
        _conv_body,
        out_shape=jax.ShapeDtypeStruct((T, D), F32),
        grid=(n_tiles,),
        in_specs=[
            pl.BlockSpec((tm, D), lambda t: (t, 0)),
            pl.BlockSpec((SUBLANES, D), lambda t: (jnp.maximum(t * sub_per_tile - 1, 0), 0)),
            pl.BlockSpec((SUBLANES, D), lambda t: (jnp.minimum((t + 1) * sub_per_tile, last_sub), 0)),
            pl.BlockSpec((None, 6, D), lambda t: (_cond_row(t, tm), 0, 0)),
            pl.BlockSpec((1, D), lambda t: (0, 0)),
            pl.BlockSpec((D, 3 * D), lambda t: (0, 0)),
            pl.BlockSpec((3, D), lambda t: (0, 0)),
            pl.BlockSpec((D, D), lambda t: (0, 0)),
        ],
        out_specs=pl.BlockSpec((tm, D), lambda t: (t, 0)),
        compiler_params=_cparams(("arbitrary",)),
        name="conv_mixer",
    )(x, x, x, mod_l, norm_g.reshape(1, D), w_in, conv_w, w_out)


def _head_norm(blk, gain):
    lane = lax.broadcasted_iota(I32, blk.shape, 1)
    low = lane < HEAD_DIM
    s = blk * blk
    s_lo = jnp.sum(jnp.where(low, s, 0.0), axis=-1, keepdims=True)
    s_hi = jnp.sum(jnp.where(low, 0.0, s), axis=-1, keepdims=True)
    ms = jnp.where(low, s_lo, s_hi) * (1.0 / HEAD_DIM)
    return blk * lax.rsqrt(ms + EPS) * gain


def _rope(blk, cos, sin_signed):
    lane = lax.broadcasted_iota(I32, blk.shape, 1)
    first = (lane & 31) < 16
    partner = jnp.where(first, pltpu.roll(blk, LANES - 16, axis=1), pltpu.roll(blk, 16, axis=1))
    return blk * cos + partner * sin_signed


def _qkv_body(*refs, nq, nk, nv, rope, emit_f32, q_scale):
    refs = list(refs)
    x_ref, mod_ref, g_ref, w_ref, qg_ref, kg_ref = refs[:6]
    refs = refs[6:]
    if rope:
        cos_ref, sin_ref = refs[:2]
        refs = refs[2:]
    q_ref, k_ref, v_ref = refs[:3]
    refs = refs[3:]
    if emit_f32:
        kf_ref, vf_ref = refs[:2]
    x = x_ref[...]
    m = mod_ref[...]
    h = _norm_mod(x, g_ref[...], m[1:2], m[0:1]).astype(BF16)
    proj = _dot(h, w_ref[...])
    if rope:
        cos = cos_ref[...]
        sin = sin_ref[...]
    for j in range(nq):
        blk = _head_norm(proj[:, j * LANES:(j + 1) * LANES], qg_ref[...])
        if rope:
            blk = _rope(blk, cos, sin)
        q_ref[:, j * LANES:(j + 1) * LANES] = (blk * q_scale).astype(BF16)
    off = nq * LANES
    for j in range(nk):
        blk = _head_norm(proj[:, off + j * LANES:off + (j + 1) * LANES], kg_ref[...])
        if emit_f32:
            kf_ref[:, j * LANES:(j + 1) * LANES] = blk
        if rope:
            blk = _rope(blk, cos, sin)
        k_ref[:, j * LANES:(j + 1) * LANES] = blk.astype(BF16)
    off = (nq + nk) * LANES
    v = proj[:, off:off + nv]
    v_ref[...] = v.astype(BF16)
    if emit_f32:
        vf_ref[...] = v


def _qkv_proj(x, mod_l, norm_g, w, q_gain, k_gain, rope_tabs, *, latent, nq, nk, nv):
    tm = TM_QKV
    n_rows = T_LAT if latent else T_CTX
    n_tiles = n_rows // tm
    tile0 = (T_CTX // tm) if latent else 0
    n_proj = (nq + nk) * LANES + nv
    gain2 = lambda g: jnp.concatenate([g, g]).reshape(1, LANES)
    in_specs = [
        pl.BlockSpec((tm, D), lambda t: (t + tile0, 0)),
        pl.BlockSpec((None, 6, D), lambda t: (_cond_row(t + tile0, tm), 0, 0)),
        pl.BlockSpec((1, D), lambda t: (0, 0)),
        pl.BlockSpec((D, n_proj), lambda t: (0, 0)),
        pl.BlockSpec((1, LANES), lambda t: (0, 0)),
        pl.BlockSpec((1, LANES), lambda t: (0, 0)),
    ]
    args = [x, mod_l, norm_g.reshape(1, D), w, gain2(q_gain), gain2(k_gain)]
    if latent:
        per_seq = DEC_SEQ // tm
        in_specs += [pl.BlockSpec((tm, LANES), lambda t: (t % per_seq, 0))] * 2
        args += list(rope_tabs)
    out_shape = [
        jax.ShapeDtypeStruct((n_rows, nq * LANES), BF16),
        jax.ShapeDtypeStruct((n_rows, nk * LANES), BF16),
        jax.ShapeDtypeStruct((n_rows, nv), BF16),
    ]
    out_specs = [
        pl.BlockSpec((tm, nq * LANES), lambda t: (t, 0)),
        pl.BlockSpec((tm, nk * LANES), lambda t: (t, 0)),
        pl.BlockSpec((tm, nv), lambda t: (t, 0)),
    ]
    if not latent:
        out_shape += [jax.ShapeDtypeStruct((n_rows, nk * LANES), F32), jax.ShapeDtypeStruct((n_rows, nv), F32)]
        out_specs += [pl.BlockSpec((tm, nk * LANES), lambda t: (t, 0)), pl.BlockSpec((tm, nv), lambda t: (t, 0))]
    body = functools.partial(_qkv_body, nq=nq, nk=nk, nv=nv, rope=latent, emit_f32=not latent,
                             q_scale=HEAD_DIM ** -0.5)
    return pl.pallas_call(
        body,
        out_shape=out_shape,
        grid=(n_tiles,),
        in_specs=in_specs,
        out_specs=out_specs,
        compiler_params=_cparams(("arbitrary",)),
        name="qkv_latent" if latent else "qkv_context",
    )(*args)


def _rope_tables():
    rows = DEC_SEQ // GRID_W
    row_pos = jnp.repeat(jnp.arange(rows, dtype=F32), GRID_W)
    col_pos = jnp.tile(jnp.arange(GRID_W, dtype=F32), rows)
    half = HEAD_DIM // 2
    inv_freq = ROPE_BASE ** (-jnp.arange(0, half, 2, dtype=F32) / half)
    ang_r = row_pos[:, None] * inv_freq[None, :]
    ang_c = col_pos[:, None] * inv_freq[None, :]
    cr, sr, cc, sc = jnp.cos(ang_r), jnp.sin(ang_r), jnp.cos(ang_c), jnp.sin(ang_c)
    cos64 = jnp.concatenate([cr, cr, cc, cc], axis=-1)
    sin64 = jnp.concatenate([-sr, sr, -sc, sc], axis=-1)
    return jnp.concatenate([cos64, cos64], axis=-1), jnp.concatenate([sin64, sin64], axis=-1)


def _swap_halves(blk):
    return pltpu.roll(blk.astype(F32), HEAD_DIM, axis=1).astype(BF16)


def _kv_operands(kh, k_blk, v_blk):
    lane = lax.broadcasted_iota(I32, k_blk.shape, 1)
    low = lane < HEAD_DIM
    k_swp = _swap_halves(k_blk)
    if kh % 2 == 0:
        k2 = jnp.where(low, k_blk, k_swp)
        v_lo = v_blk
    else:
        k2 = jnp.where(low, k_swp, k_blk)
        v_lo = _swap_halves(v_blk)
    v2 = jnp.where(low, v_lo, jnp.ones_like(v_lo))
    return k2, v2


def _stack_query_heads(q_ref, kh, tq):
    lane = lax.broadcasted_iota(I32, (tq, LANES), 1)
    low = lane < HEAD_DIM
    parts = []
    for pair in range(SWA_GROUP // 2):
        blk = (SWA_GROUP // 2) * kh + pair
        qp = q_ref[:, blk * LANES:(blk + 1) * LANES]
        zero = jnp.zeros_like(qp)
        parts += [jnp.where(low, qp, zero), jnp.where(low, zero, qp)]
    return jnp.concatenate(parts, axis=0)


def _sink_column(sink_ref, kh, tq):
    row = lax.broadcasted_iota(I32, (SWA_GROUP * tq, 1), 0)
    col = jnp.full((SWA_GROUP * tq, 1), sink_ref[SWA_GROUP * kh], F32)
    for g in range(1, SWA_GROUP):
        col = jnp.where(row >= g * tq, sink_ref[SWA_GROUP * kh + g], col)
    return col


def _store_head_group(att_ref, kh, tq, acc, sink_term):
    lane = lax.broadcasted_iota(I32, (tq, LANES), 1)
    low = lane < HEAD_DIM
    den = acc[:, HEAD_DIM:HEAD_DIM + 1] + sink_term
    r = acc / den
    r_hi = pltpu.roll(r, HEAD_DIM, axis=1)
    for pair in range(SWA_GROUP // 2):
        blk = (SWA_GROUP // 2) * kh + pair
        a = r[(2 * pair) * tq:(2 * pair + 1) * tq]
        b = r_hi[(2 * pair + 1) * tq:(2 * pair + 2) * tq]
        att_ref[:, blk * LANES:(blk + 1) * LANES] = jnp.where(low, a, b).astype(BF16)


def _swa_ctx_body(sink_ref, x_ref, mod_ref, q_ref, k_ref, v_ref, wout_ref, o_ref, att_ref):
    tq = q_ref.shape[0]
    for kh in range(SWA_KV_HEADS):
        blk = kh // 2
        k2, v2 = _kv_operands(kh, k_ref[:, blk * LANES:(blk + 1) * LANES], v_ref[:, blk * LANES:(blk + 1) * LANES])
        qs = _stack_query_heads(q_ref, kh, tq)
        sink = _sink_column(sink_ref, kh, tq)
        s = _dot_nt(qs, k2)
        mx = jnp.maximum(jnp.max(s, axis=-1, keepdims=True), sink)
        p = jnp.exp(s - mx)
        acc = _dot(p.astype(BF16), v2)
        _store_head_group(att_ref, kh, tq, acc, jnp.exp(sink - mx))
    gate = mod_ref[2:3, :]
    o_ref[...] = x_ref[...] + gate * _dot(att_ref[...], wout_ref[...])


def _swa_ctx_attention(x, mod_l, q, k, v, sink, w_out):
    nkv = SWA_KV_HEADS * HEAD_DIM
    return pl.pallas_call(
        _swa_ctx_body,
        out_shape=jax.ShapeDtypeStruct((T, D), F32),
        grid=(BATCH,),
        in_specs=[
            pl.BlockSpec(memory_space=pltpu.SMEM),
            pl.BlockSpec((SEQ, D), lambda b: (b, 0)),
            pl.BlockSpec((None, 6, D), lambda b: (0, 0, 0)),
            pl.BlockSpec((SEQ, D), lambda b: (b, 0)),
            pl.BlockSpec((SEQ, nkv), lambda b: (b, 0)),
            pl.BlockSpec((SEQ, nkv), lambda b: (b, 0)),
            pl.BlockSpec((D, D), lambda b: (0, 0)),
        ],
        out_specs=pl.BlockSpec((SEQ, D), lambda b: (b, 0)),
        scratch_shapes=[pltpu.VMEM((SEQ, D), BF16)],
        input_output_aliases={1: 0},
        compiler_params=_cparams(("arbitrary",)),
        name="swa_context_attention",
    )(sink, x, mod_l, q, k, v, w_out)


SWA_TQ = 128
SWA_KWIN = 3 * SWA_TQ


def _swa_lat_body(sink_ref, x_ref, mod_ref, q_ref, k_ref, v_ref, kc_ref, vc_ref, wout_ref, o_ref, att_ref):
    j = pl.program_id(1)
    tq = SWA_TQ
    start = jnp.clip((j - 1) * tq, 0, DEC_SEQ - SWA_KWIN)
    start = pl.multiple_of(start, tq)
    row = lax.broadcasted_iota(I32, (SWA_GROUP * tq, 1), 0)
    q_pos = j * tq + (row & (tq - 1))
    k_pos = start + lax.broadcasted_iota(I32, (1, SWA_KWIN), 1)
    valid = jnp.abs(k_pos - q_pos) <= SWA_WINDOW
    for kh in range(SWA_KV_HEADS):
        blk = kh // 2
        lanes = slice(blk * LANES, (blk + 1) * LANES)
        k2, v2 = _kv_operands(kh, k_ref[pl.ds(start, SWA_KWIN), lanes], v_ref[pl.ds(start, SWA_KWIN), lanes])
        kc2, vc2 = _kv_operands(kh, kc_ref[:, lanes], vc_ref[:, lanes])
        qs = _stack_query_heads(q_ref, kh, tq)
        sink = _sink_column(sink_ref, kh, tq)
        s_lat = jnp.where(valid, _dot_nt(qs, k2), -jnp.inf)
        s_ctx = _dot_nt(qs, kc2)
        mx = jnp.maximum(jnp.max(s_lat, axis=-1, keepdims=True), jnp.max(s_ctx, axis=-1, keepdims=True))
        mx = jnp.maximum(mx, sink)
        p_lat = jnp.exp(s_lat - mx).astype(BF16)
        p_ctx = jnp.exp(s_ctx - mx).astype(BF16)
        acc = _dot(p_lat, v2) + _dot(p_ctx, vc2)
        _store_head_group(att_ref, kh, tq, acc, jnp.exp(sink - mx))
    gate = mod_ref[2:3, :]
    o_ref[...] = x_ref[...] + gate * _dot(att_ref[...], wout_ref[...])


def _swa_lat_attention(x, mod_l, q, k, v, k_cache, v_cache, sink, w_out):
    nkv = SWA_KV_HEADS * HEAD_DIM
    nb = DEC_SEQ // SWA_TQ
    x_tile0 = T_CTX // SWA_TQ
    return pl.pallas_call(
        _swa_lat_body,
        out_shape=jax.ShapeDtypeStruct((T, D), F32),
        grid=(DEC_BATCH, nb),
        in_specs=[
            pl.BlockSpec(memory_space=pltpu.SMEM),
            pl.BlockSpec((SWA_TQ, D), lambda b, j: (x_tile0 + b * nb + j, 0)),
            pl.BlockSpec((None, 6, D), lambda b, j: (1 + b, 0, 0)),
            pl.BlockSpec((SWA_TQ, D), lambda b, j: (b * nb + j, 0)),
            pl.BlockSpec((DEC_SEQ, nkv), lambda b, j: (b, 0)),
            pl.BlockSpec((DEC_SEQ, nkv), lambda b, j: (b, 0)),
            pl.BlockSpec((None, PAST_LEN, nkv), lambda b, j: (b, 0, 0)),
            pl.BlockSpec((None, PAST_LEN, nkv), lambda b, j: (b, 0, 0)),
            pl.BlockSpec((D, D), lambda b, j: (0, 0)),
        ],
        out_specs=pl.BlockSpec((SWA_TQ, D), lambda b, j: (x_tile0 + b * nb + j, 0)),
        scratch_shapes=[pltpu.VMEM((SWA_TQ, D), BF16)],
        input_output_aliases={1: 0},
        compiler_params=_cparams(("arbitrary", "arbitrary")),
        name="swa_latent_attention",
    )(sink, x, mod_l, q, k, v, k_cache, v_cache, w_out)


DIFF_TQ = 256


def _diff_body(x_ref, mod_ref, q_ref, k_ref, v_ref, lam_ref, sg_ref, wout_ref, o_ref, att_ref, *, lam_init):
    tq = q_ref.shape[0]
    lane = lax.broadcasted_iota(I32, (tq, LANES), 1)
    low = lane < HEAD_DIM
    lv = lam_ref[...]
    lam = (jnp.exp(jnp.sum(lv[0:1] * lv[1:2], axis=-1, keepdims=True))
           - jnp.exp(jnp.sum(lv[2:3] * lv[3:4], axis=-1, keepdims=True)) + lam_init)
    sub_gain = sg_ref[...]
    for hd in range(DIFF_HEADS):
        lanes = slice(hd * LANES, (hd + 1) * LANES)
        qh = q_ref[:, lanes]
        vh = v_ref[:, lanes]
        zero = jnp.zeros_like(qh)
        q2 = jnp.concatenate([jnp.where(low, qh, zero), jnp.where(low, zero, qh)], axis=0)
        s = _dot_nt(q2, k_ref[:, lanes])
        p = jnp.exp(s - jnp.max(s, axis=-1, keepdims=True)).astype(BF16)
        acc = _dot(p, jnp.concatenate([vh, jnp.ones_like(vh)], axis=1))
        r = acc[:, :LANES] / acc[:, LANES:LANES + 1]
        o = r[:tq] - lam * r[tq:]
        ms = jnp.mean(o * o, axis=-1, keepdims=True)
        o = (o * lax.rsqrt(ms + EPS) * sub_gain) * (1.0 - lam_init)
        att_ref[:, lanes] = o.astype(BF16)
    gate = mod_ref[2:3, :]
    o_ref[...] = x_ref[...] + gate * _dot(att_ref[...], wout_ref[...])


def _diff_attention(x, mod_l, q, k_all, v_all, lam_vecs, sub_gain, w_out, *, latent, lam_init):
    tq = DIFF_TQ
    n_keys = k_all.shape[1]
    if latent:
        per_batch = DEC_SEQ // tq
        grid = (DEC_BATCH, per_batch)
        x_tile0 = T_CTX // tq
        x_map = lambda b, j: (x_tile0 + b * per_batch + j, 0)
        q_map = lambda b, j: (b * per_batch + j, 0)
        mod_map = lambda b, j: (1 + b, 0, 0)
    else:
        per_batch = SEQ // tq
        grid = (BATCH, per_batch)
        x_map = lambda b, j: (b * per_batch + j, 0)
        q_map = x_map
        mod_map = lambda b, j: (0, 0, 0)
    const2 = lambda b, j: (0, 0)
    return pl.pallas_call(
        functools.partial(_diff_body, lam_init=lam_init),
        out_shape=jax.ShapeDtypeStruct((T, D), F32),
        grid=grid,
        in_specs=[
            pl.BlockSpec((tq, D), x_map),
            pl.BlockSpec((None, 6, D), mod_map),
            pl.BlockSpec((tq, D), q_map),
            pl.BlockSpec((None, n_keys, D), lambda b, j: (b, 0, 0)),
            pl.BlockSpec((None, n_keys, D), lambda b, j: (b, 0, 0)),
            pl.BlockSpec((4, HEAD_DIM), const2),
            pl.BlockSpec((1, 2 * HEAD_DIM), const2),
            pl.BlockSpec((D, D), const2),
        ],
        out_specs=pl.BlockSpec((tq, D), x_map),
        scratch_shapes=[pltpu.VMEM((tq, D), BF16)],
        input_output_aliases={0: 0},
        compiler_params=_cparams(("arbitrary", "arbitrary")),
        name="diff_latent_attention" if latent else "diff_context_attention",
    )(x, mod_l, q, k_all, v_all, lam_vecs, sub_gain, w_out)


def _dispatch_body(x_ref, mod_ref, g_ref, wr_hi_ref, wr_lo_ref, br_ref, ltri_ref, utri_ref,
                   z_ref, pg_ref, meta_ref):
    tm = TM_MOE
    m = mod_ref[...]
    h = _norm_mod(x_ref[...], g_ref[...], m[4:5], m[3:4])
    hb = h.astype(BF16)
    h_lo = (h - hb.astype(F32)).astype(BF16)
    wr_hi = wr_hi_ref[...]
    logits = _dot(hb, wr_hi) + _dot(h_lo, wr_hi) + _dot(hb, wr_lo_ref[...]) + br_ref[...]
    lane = lax.broadcasted_iota(I32, (tm, LANES), 1)
    work = jnp.where(lane < N_EXPERTS, logits, -jnp.inf)
    picks, vals = [], []
    for _ in range(TOP_K):
        best = jnp.max(work, axis=-1, keepdims=True)
        idx = jnp.min(jnp.where(work == best, lane, LANES), axis=-1, keepdims=True)
        hit = lane == idx
        work = jnp.where(hit, -jnp.inf, work)
        picks.append(hit)
        vals.append(best)
    exps = [jnp.exp(v - vals[0]) for v in vals]
    den = exps[0] + exps[1] + exps[2] + exps[3]
    gates = [e / den for e in exps]
    chosen = jnp.where(picks[0] | picks[1] | picks[2] | picks[3], 1.0, 0.0)
    chosen_b = chosen.astype(BF16)

    rank = _dot(ltri_ref[...], chosen_b)
    count = jnp.sum(chosen, axis=0, keepdims=True)
    chunks = jnp.floor((count + (CHUNK - 1)) * (1.0 / CHUNK))
    chunks8 = jnp.broadcast_to(chunks, (SUBLANES, LANES))
    start = _dot(chunks8.astype(BF16), utri_ref[...])
    start_row = start[0:1]
    pos_full = start_row * CHUNK + rank

    pg = jnp.zeros((tm, LANES), F32)
    for k in range(TOP_K):
        pos_k = jnp.sum(jnp.where(picks[k], pos_full, 0.0), axis=-1, keepdims=True)
        pg = jnp.where(lane == k, pos_k, pg)
        pg = jnp.where(lane == TOP_K + k, gates[k], pg)
    pg_ref[...] = pg

    ones8 = jnp.ones((SUBLANES, LANES), BF16)
    chunk_part = jnp.concatenate([jnp.where(p, start_row, 0.0) for p in picks], axis=0).astype(BF16)
    rank_part = jnp.concatenate([jnp.where(p, rank, 0.0) for p in picks], axis=0).astype(BF16)
    pos_row = (_dot_nt(ones8, chunk_part) * CHUNK + _dot_nt(ones8, rank_part))[0:1]
    row_id = lax.broadcasted_iota(I32, (LOCAL_ROWS, tm), 0).astype(F32)
    sel_t = jnp.zeros((LOCAL_ROWS, tm), F32)
    for k in range(TOP_K):
        sel_t = jnp.where(row_id == pos_row[:, k * tm:(k + 1) * tm], 1.0, sel_t)
    rows = _dot(sel_t.astype(BF16), hb)
    z_ref[...] = _chunks_to_storage(rows)

    sub = lax.broadcasted_iota(I32, (SUBLANES, LANES), 0)
    meta = jnp.where(sub == 0, chunks8, jnp.where(sub == 1, start, 0.0))
    meta_ref[0] = meta.astype(I32)


def _moe_dispatch(x, mod_l, norm_g, w_router, b_router):
    wr = jnp.pad(w_router, ((0, 0), (0, LANES - N_EXPERTS)))
    wr_hi = wr.astype(BF16)
    wr_lo = (wr - wr_hi.astype(F32)).astype(BF16)
    br = jnp.pad(b_router, (0, LANES - N_EXPERTS)).reshape(1, LANES)
    r = jnp.arange(TM_MOE)
    ltri = (r[None, :] < r[:, None]).astype(BF16)
    c = jnp.arange(LANES)
    utri = (c[:, None] < c[None, :]).astype(BF16)
    const2 = lambda t: (0, 0)
    return pl.pallas_call(
        _dispatch_body,
        out_shape=[
            jax.ShapeDtypeStruct((N_TILE_MOE * LOCAL_CHUNKS, 2 * CHUNK, HALF), BF16),
            jax.ShapeDtypeStruct((T, LANES), F32),
            jax.ShapeDtypeStruct((N_TILE_MOE, SUBLANES, LANES), I32),
        ],
        grid=(N_TILE_MOE,),
        in_specs=[
            pl.BlockSpec((TM_MOE, D), lambda t: (t, 0)),
            pl.BlockSpec((None, 6, D), lambda t: (_cond_row(t, TM_MOE), 0, 0)),
            pl.BlockSpec((1, D), const2),
            pl.BlockSpec((D, LANES), const2),
            pl.BlockSpec((D, LANES), const2),
            pl.BlockSpec((1, LANES), const2),
            pl.BlockSpec((TM_MOE, TM_MOE), const2),
            pl.BlockSpec((LANES, LANES), const2),
        ],
        out_specs=[
            pl.BlockSpec((LOCAL_CHUNKS, 2 * CHUNK, HALF), lambda t: (t, 0, 0)),
            pl.BlockSpec((TM_MOE, LANES), lambda t: (t, 0)),
            pl.BlockSpec((1, SUBLANES, LANES), lambda t: (t, 0, 0)),
        ],
        compiler_params=_cparams(("arbitrary",)),
        name="moe_dispatch",
    )(x, mod_l, norm_g.reshape(1, D), wr_hi, wr_lo, br, ltri, utri)


def _expert_body(cnt_ref, first_ref, tstart_ref, ntile_ref, z_hbm, wgu_ref, bgu_ref, wd_ref, bd_ref, perm_ref,
                 y_hbm, wgu_s, wd_s, xbuf, ybuf, src_tab, sem_in, sem_out):
    e = pl.program_id(0)
    t0 = tstart_ref[e]
    nt = ntile_ref[e]
    total = tstart_ref[N_EXPERTS - 1] + ntile_ref[N_EXPERTS - 1]

    def build_source_table():
        def clear(p, carry):
            src_tab[p] = 0
            return carry

        def per_expert(ex, carry):
            def per_token_tile(tt, pos):
                n = cnt_ref[tt * N_EXPERTS + ex]
                first = tt * LOCAL_CHUNKS + first_ref[tt * N_EXPERTS + ex]

                def per_chunk(c, inner):
                    src_tab[pos + c] = first + c
                    return inner

                lax.fori_loop(0, n, per_chunk, 0)
                return pos + n

            pos = lax.fori_loop(0, N_TILE_MOE, per_token_tile, tstart_ref[ex] * EXP_CHUNKS)
            lax.fori_loop(pos, (tstart_ref[ex] + ntile_ref[ex]) * EXP_CHUNKS, clear, 0)
            return carry

        lax.fori_loop(0, N_EXPERTS, per_expert, 0)

    def start_gather(g, slot):
        base = jnp.minimum(g, total - 1) * EXP_CHUNKS
        for c in range(EXP_CHUNKS):
            pltpu.make_async_copy(z_hbm.at[src_tab[base + c]], xbuf.at[slot, c], sem_in.at[slot]).start()

    def wait_gather(slot):
        pltpu.make_async_copy(z_hbm.at[pl.ds(0, EXP_CHUNKS)], xbuf.at[slot], sem_in.at[slot]).wait()

    def out_copy(g, slot):
        return pltpu.make_async_copy(ybuf.at[slot], y_hbm.at[pl.ds(g * EXP_CHUNKS, EXP_CHUNKS)],
                                     sem_out.at[slot])

    @pl.when(e == 0)
    def _():
        build_source_table()
        for g in range(N_XBUF - 1):
            start_gather(g, g)

    @pl.when(nt > 0)
    def _():
        for blk in range(2 * D_EXPERT // MXU_DIM):
            cols = slice(blk * MXU_DIM, (blk + 1) * MXU_DIM)
            wgu_s[:, cols] = _dot(wgu_ref[:, cols].astype(BF16), perm_ref[...]).astype(BF16)
        wd_s[...] = wd_ref[...].astype(BF16)

    def tile_body(i, carry):
        g = t0 + i
        slot = lax.rem(g, N_XBUF)
        oslot = lax.rem(g, 2)
        start_gather(g + N_XBUF - 1, lax.rem(g + N_XBUF - 1, N_XBUF))
        wait_gather(slot)
        x = _chunks_from_storage(xbuf[slot])
        acts = []
        for blk in range(2 * D_EXPERT // MXU_DIM):
            cols = slice(blk * MXU_DIM, (blk + 1) * MXU_DIM)
            gu = _dot(x, wgu_s[:, cols]) + bgu_ref[:, cols]
            x_glu = jnp.minimum(gu[:, :LANES], SWIGLU_LIMIT)
            x_lin = jnp.clip(gu[:, LANES:], -SWIGLU_LIMIT, SWIGLU_LIMIT)
            acts.append((x_glu * jax.nn.sigmoid(SWIGLU_ALPHA * x_glu) * (x_lin + 1.0)).astype(BF16))
        y = _dot(jnp.concatenate(acts, axis=1), wd_s[...]) + bd_ref[...]

        @pl.when(g >= 2)
        def _():
            out_copy(g - 2, oslot).wait()

        ybuf[oslot] = _chunks_to_storage(y)
        out_copy(g, oslot).start()
        return carry

    lax.fori_loop(0, nt, tile_body, 0)

    @pl.when(e == N_EXPERTS - 1)
    def _():
        for ahead in range(N_XBUF - 1):
            wait_gather(lax.rem(total + ahead, N_XBUF))
        last = total - 1
        out_copy(last, lax.rem(last, 2)).wait()

        @pl.when(total >= 2)
        def _():
            out_copy(last - 1, lax.rem(last - 1, 2)).wait()

        ybuf[0] = jnp.zeros((EXP_CHUNKS, 2 * CHUNK, HALF), BF16)

        def start_fill(g, carry):
            out_copy(g, 0).start()
            return carry

        def wait_fill(g, carry):
            out_copy(g, 0).wait()
            return carry

        lax.fori_loop(total, MAX_EXP_TILES, start_fill, 0)
        lax.fori_loop(total, MAX_EXP_TILES, wait_fill, 0)


def _moe_experts(z, cnt, first, tstart, ntile, layer, w_gate_up, bgu_all, w_down, b_down):
    n_gu = 2 * D_EXPERT
    i = jnp.arange(MXU_DIM)
    perm = (i[:, None] == jnp.where(i < LANES, 2 * i, 2 * (i - LANES) + 1)[None, :]).astype(BF16)
    wmap = lambda e, *_: (layer, e, 0, 0)
    grid_spec = pltpu.PrefetchScalarGridSpec(
        num_scalar_prefetch=4,
        grid=(N_EXPERTS,),
        in_specs=[
            pl.BlockSpec(memory_space=pl.ANY),
            pl.BlockSpec((None, None, D, n_gu), wmap),
            pl.BlockSpec((None, None, 1, n_gu), wmap),
            pl.BlockSpec((None, None, D_EXPERT, D), wmap),
            pl.BlockSpec((None, None, 1, D), wmap),
            pl.BlockSpec((MXU_DIM, MXU_DIM), lambda e, *_: (0, 0)),
        ],
        out_specs=pl.BlockSpec(memory_space=pl.ANY),
        scratch_shapes=[
            pltpu.VMEM((D, n_gu), BF16),
            pltpu.VMEM((D_EXPERT, D), BF16),
            pltpu.VMEM((N_XBUF, EXP_CHUNKS, 2 * CHUNK, HALF), BF16),
            pltpu.VMEM((2, EXP_CHUNKS, 2 * CHUNK, HALF), BF16),
            pltpu.SMEM((MAX_EXP_CHUNKS,), I32),
            pltpu.SemaphoreType.DMA((N_XBUF,)),
            pltpu.SemaphoreType.DMA((2,)),
        ],
    )
    return pl.pallas_call(
        _expert_body,
        out_shape=jax.ShapeDtypeStruct((MAX_EXP_CHUNKS, 2 * CHUNK, HALF), BF16),
        grid_spec=grid_spec,
        compiler_params=_cparams(("arbitrary",)),
        name="moe_experts",
    )(cnt, first, tstart, ntile, z, w_gate_up, bgu_all, w_down,
      b_down.reshape(DEPTH, N_EXPERTS, 1, D), perm)


def _combine_body(src_ref, x_ref, mod_ref, pg_ref, y_hbm, o_ref, ybuf, sem):
    t = pl.program_id(0)
    n_t = pl.num_programs(0)
    slot = lax.rem(t, 2)

    def start_gather(tile, sl):
        for c in range(LOCAL_CHUNKS):
            pltpu.make_async_copy(y_hbm.at[src_ref[tile * LOCAL_CHUNKS + c]], ybuf.at[sl, c], sem.at[sl]).start()

    def wait_gather(sl):
        pltpu.make_async_copy(y_hbm.at[pl.ds(0, LOCAL_CHUNKS)], ybuf.at[sl], sem.at[sl]).wait()

    @pl.when(t == 0)
    def _():
        start_gather(0, 0)

    start_gather(jnp.minimum(t + 1, n_t - 1), 1 - slot)
    wait_gather(slot)
    y = _chunks_from_storage(ybuf[slot])
    pg = pg_ref[...]
    col = lax.broadcasted_iota(I32, (TM_MOE, LOCAL_ROWS), 1).astype(F32)
    comb = jnp.zeros((TM_MOE, LOCAL_ROWS), F32)
    for k in range(TOP_K):
        comb = jnp.where(col == pg[:, k:k + 1], pg[:, TOP_K + k:TOP_K + k + 1], comb)
    moe = _dot(comb.astype(BF16), y)
    o_ref[...] = x_ref[...] + mod_ref[5:6, :] * moe

    @pl.when(t == n_t - 1)
    def _():
        wait_gather(1 - slot)


def _moe_combine(x, mod_l, pg, y, src):
    grid_spec = pltpu.PrefetchScalarGridSpec(
        num_scalar_prefetch=1,
        grid=(N_TILE_MOE,),
        in_specs=[
            pl.BlockSpec((TM_MOE, D), lambda t, *_: (t, 0)),
            pl.BlockSpec((None, 6, D), lambda t, *_: (_cond_row(t, TM_MOE), 0, 0)),
            pl.BlockSpec((TM_MOE, LANES), lambda t, *_: (t, 0)),
            pl.BlockSpec(memory_space=pl.ANY),
        ],
        out_specs=pl.BlockSpec((TM_MOE, D), lambda t, *_: (t, 0)),
        scratch_shapes=[
            pltpu.VMEM((2, LOCAL_CHUNKS, 2 * CHUNK, HALF), BF16),
            pltpu.SemaphoreType.DMA((2,)),
        ],
    )
    return pl.pallas_call(
        _combine_body,
        out_shape=jax.ShapeDtypeStruct((T, D), F32),
        grid_spec=grid_spec,
        input_output_aliases={1: 0},
        compiler_params=_cparams(("arbitrary",)),
        name="moe_combine",
    )(src, x, mod_l, pg, y)


def _moe_layer(x, mod_l, norm_g, layer, w_router, b_router, w_gate_up, bgu_all, w_down, b_down):
    z, pg, meta = _moe_dispatch(x, mod_l, norm_g, w_router, b_router)
    cnt = meta[:, 0, :N_EXPERTS]
    first = meta[:, 1, :N_EXPERTS]
    ntile = (jnp.sum(cnt, axis=0) + EXP_CHUNKS - 1) // EXP_CHUNKS
    tstart = jnp.cumsum(ntile) - ntile
    y = _moe_experts(z, cnt.reshape(-1), first.reshape(-1), tstart, ntile, layer,
                     w_gate_up, bgu_all, w_down, b_down)
    end = first + cnt
    seen = jnp.cumsum(cnt, axis=0) - cnt
    j = jnp.arange(LOCAL_CHUNKS, dtype=I32)
    e_of_j = jnp.sum((end[:, None, :] <= j[None, :, None]).astype(I32), axis=-1)
    used = j[None, :] < end[:, -1:]
    own = jnp.arange(N_EXPERTS, dtype=I32)[None, None, :] == e_of_j[:, :, None]
    offset = tstart[None, :] * EXP_CHUNKS + seen - first
    src = j[None, :] + jnp.sum(jnp.where(own, offset[:, None, :], 0), axis=-1)
    src = jnp.where(used, src, 0).astype(I32).reshape(-1)
    return _moe_combine(x, mod_l, pg, y, src)


def kernel(x_prompt, x_sample, c, cache_swa_k, cache_swa_v, cache_diff_k, cache_diff_v, c_ctx, norm_mix_g, norm_ffn_g, ada_w, ada_b, conv_w_in, conv_w, conv_w_out, swa_w_qkv, swa_q_gain, swa_k_gain, swa_sink, swa_w_out, diff_w_qkv, diff_q_gain, diff_k_gain, diff_lam_q1, diff_lam_k1, diff_lam_q2, diff_lam_k2, diff_sub_gain, diff_w_out, moe_w_router, moe_b_router, moe_w_gate_up, moe_b_gate_up, moe_w_down, moe_b_down):
    x = jnp.concatenate([x_prompt.reshape(T_CTX, D), x_sample.reshape(T_LAT, D)], axis=0)
    cond = jnp.concatenate([c_ctx[None, :], c, jnp.zeros((COND_PAD - N_COND, D), F32)], axis=0)
    mod = _modulation(cond, ada_w, ada_b).reshape(DEPTH, COND_PAD, 6, D)
    rope_tabs = _rope_tables()
    n_gu = 2 * D_EXPERT
    bgu_all = moe_b_gate_up.reshape(DEPTH, N_EXPERTS, n_gu // MXU_DIM, LANES, 2)
    bgu_all = bgu_all.transpose(0, 1, 2, 4, 3).reshape(DEPTH, N_EXPERTS, 1, n_gu)
    new_kv = {}
    for i in range(DEPTH):
        j = i // 3
        kind = i % 3
        mod_l = mod[i]
        if kind == 0:
            x = _conv_layer(x, mod_l, norm_mix_g[i], conv_w_in[j].astype(BF16), conv_w[j],
                            conv_w_out[j].astype(BF16))
        elif kind == 1:
            w = swa_w_qkv[j].astype(BF16)
            w_out = swa_w_out[j].astype(BF16)
            nkv = SWA_KV_HEADS * HEAD_DIM
            dims = dict(nq=SWA_Q_HEADS * HEAD_DIM // LANES, nk=nkv // LANES, nv=nkv)
            q, k, v, kf, vf = _qkv_proj(x, mod_l, norm_mix_g[i], w, swa_q_gain[j], swa_k_gain[j], None,
                                        latent=False, **dims)
            new_kv["swa_k"] = kf.reshape(BATCH, 1, SEQ, SWA_KV_HEADS, HEAD_DIM)
            new_kv["swa_v"] = vf.reshape(BATCH, 1, SEQ, SWA_KV_HEADS, HEAD_DIM)
            x = _swa_ctx_attention(x, mod_l, q, k, v, swa_sink[j], w_out)
            q, k, v = _qkv_proj(x, mod_l, norm_mix_g[i], w, swa_q_gain[j], swa_k_gain[j], rope_tabs,
                                latent=True, **dims)
            kc = cache_swa_k[:, j].reshape(DEC_BATCH, PAST_LEN, nkv).astype(BF16)
            vc = cache_swa_v[:, j].reshape(DEC_BATCH, PAST_LEN, nkv).astype(BF16)
            x = _swa_lat_attention(x, mod_l, q, k, v, kc, vc, swa_sink[j], w_out)
        else:
            w = diff_w_qkv[j].astype(BF16)
            w_out = diff_w_out[j].astype(BF16)
            lam_init = 0.8 - 0.6 * math.exp(-0.3 * i)
            lam_vecs = jnp.stack([diff_lam_q1[j], diff_lam_k1[j], diff_lam_q2[j], diff_lam_k2[j]])
            sub_gain = diff_sub_gain[j].reshape(1, 2 * HEAD_DIM)
            dims = dict(nq=D // LANES, nk=D // LANES, nv=D)
            q, k, v, kf, vf = _qkv_proj(x, mod_l, norm_mix_g[i], w, diff_q_gain[j], diff_k_gain[j], None,
                                        latent=False, **dims)
            new_kv["diff_k"] = kf.reshape(BATCH, 1, SEQ, DIFF_HEADS, 2, HEAD_DIM)
            new_kv["diff_v"] = vf.reshape(BATCH, 1, SEQ, DIFF_HEADS, 2 * HEAD_DIM)
            x = _diff_attention(x, mod_l, q, k.reshape(BATCH, SEQ, D), v.reshape(BATCH, SEQ, D), lam_vecs,
                                sub_gain, w_out, latent=False, lam_init=lam_init)
            q, k, v = _qkv_proj(x, mod_l, norm_mix_g[i], w, diff_q_gain[j], diff_k_gain[j], rope_tabs,
                                latent=True, **dims)
            kc = cache_diff_k[:, j].reshape(DEC_BATCH, PAST_LEN, D).astype(BF16)
            vc = cache_diff_v[:, j].reshape(DEC_BATCH, PAST_LEN, D).astype(BF16)
            k_all = jnp.concatenate([k.reshape(DEC_BATCH, DEC_SEQ, D), kc], axis=1)
            v_all = jnp.concatenate([v.reshape(DEC_BATCH, DEC_SEQ, D), vc], axis=1)
            x = _diff_attention(x, mod_l, q, k_all, v_all, lam_vecs, sub_gain, w_out, latent=True,
                                lam_init=lam_init)
        x = _moe_layer(x, mod_l, norm_ffn_g[i], i, moe_w_router[i], moe_b_router[i], moe_w_gate_up,
                       bgu_all, moe_w_down, moe_b_down)
    y_prompt = x[:T_CTX].reshape(BATCH, SEQ, D)
    y_sample = x[T_CTX:].reshape(DEC_BATCH, DEC_SEQ, D)
    return (y_prompt, y_sample, new_kv["swa_k"], new_kv["swa_v"], new_kv["diff_k"], new_kv["diff_v"])
```

```python
import functools
import math

import jax
import jax.numpy as jnp
from jax import lax
from jax.experimental import pallas as pl
from jax.experimental.pallas import tpu as pltpu

F32 = jnp.float32
BF16 = jnp.bfloat16
I32 = jnp.int32

D = 1024
BATCH = 32
SEQ = 256
DEC_BATCH = 4
DEC_SEQ = 2048
PAST_LEN = 512
DEPTH = 4
GRID_W = 64
HEAD_DIM = 64
SWA_Q_HEADS = 16
SWA_KV_HEADS = 4
SWA_GROUP = SWA_Q_HEADS // SWA_KV_HEADS
SWA_WINDOW = 128
DIFF_HEADS = 8
N_EXPERTS = 32
TOP_K = 4
D_EXPERT = 1024
SWIGLU_ALPHA = 1.702
SWIGLU_LIMIT = 7.0
ROPE_BASE = 10000.0
EPS = 1e-6

T_CTX = BATCH * SEQ
T_LAT = DEC_BATCH * DEC_SEQ
T = T_CTX + T_LAT
N_COND = 1 + DEC_BATCH
COND_PAD = 8

LANES = 128
SUBLANES = 8
BF16_ROWS = 16
MXU_DIM = 256

TM_CONV = 512
TM_QKV = 512
TM_MOE = 256
TM_EXP = 256
CHUNK = SUBLANES
HALF = D // 2
assert 2 * CHUNK == BF16_ROWS
N_TILE_MOE = T // TM_MOE
LOCAL_ROWS = TM_MOE * TOP_K + N_EXPERTS * (CHUNK - 1)
LOCAL_ROWS = -(-LOCAL_ROWS // LANES) * LANES
LOCAL_CHUNKS = LOCAL_ROWS // CHUNK
EXP_CHUNKS = TM_EXP // CHUNK
MAX_GROUP_ROWS = T * TOP_K + N_TILE_MOE * N_EXPERTS * (CHUNK - 1)
MAX_EXP_TILES = -(-MAX_GROUP_ROWS // TM_EXP) + N_EXPERTS
MAX_EXP_CHUNKS = MAX_EXP_TILES * EXP_CHUNKS
N_XBUF = 3
N_YBUF = 3


def _chunks_to_storage(rows):
    r3 = rows.reshape(rows.shape[0] // CHUNK, CHUNK, D)
    return jnp.concatenate([r3[:, :, :HALF], r3[:, :, HALF:]], axis=1).astype(BF16)


def _chunks_from_storage(stored):
    s = stored.astype(F32)
    r3 = jnp.concatenate([s[:, :CHUNK, :], s[:, CHUNK:, :]], axis=-1)
    return r3.reshape(stored.shape[0] * CHUNK, D).astype(BF16)

VMEM_LIMIT = 56 * 1024 * 1024


def _cparams(sem, vmem=VMEM_LIMIT):
    return pltpu.CompilerParams(dimension_semantics=sem, vmem_limit_bytes=vmem)


def _dot(a, b):
    return jnp.dot(a, b, preferred_element_type=F32)


def _dot_nt(a, b):
    return lax.dot_general(a, b, (((1,), (1,)), ((), ())), preferred_element_type=F32)


def _dot_tn(a, b):
    return lax.dot_general(a, b, (((0,), (0,)), ((), ())), preferred_element_type=F32)


def _norm_mod(x, g, scale, shift):
    ms = jnp.mean(x * x, axis=-1, keepdims=True)
    y = x * lax.rsqrt(ms + EPS)
    return (y * g) * (1.0 + scale) + shift


MOD_TN = 1536


def _mod_body(cond_ref, w_ref, b_ref, o_ref):
    c = cond_ref[...]
    s = (c * jax.nn.sigmoid(c)).astype(BF16)
    o_ref[0] = _dot(s, w_ref[0].astype(BF16)) + b_ref[0]


def _modulation(cond, ada_w, ada_b):
    n_layer = ada_w.shape[0]
    n_out = ada_w.shape[2]
    return pl.pallas_call(
        _mod_body,
        out_shape=jax.ShapeDtypeStruct((n_layer, COND_PAD, n_out), F32),
        grid=(n_layer, n_out // MOD_TN),
        in_specs=[
            pl.BlockSpec((COND_PAD, D), lambda i, j: (0, 0)),
            pl.BlockSpec((1, D, MOD_TN), lambda i, j: (i, 0, j)),
            pl.BlockSpec((1, 1, MOD_TN), lambda i, j: (i, 0, j)),
        ],
        out_specs=pl.BlockSpec((1, COND_PAD, MOD_TN), lambda i, j: (i, 0, j)),
        compiler_params=_cparams(("arbitrary", "arbitrary")),
        name="modulation",
    )(cond, ada_w, ada_b.reshape(n_layer, 1, n_out))


def _cond_row(row_tile, tile_rows):
    ctx_tiles = T_CTX // tile_rows
    per_batch = DEC_SEQ // tile_rows
    return jnp.where(row_tile < ctx_tiles, 0, 1 + (row_tile - ctx_tiles) // per_batch)


def _conv_body(x_ref, xp_ref, xn_ref, mod_ref, g_ref, win_ref, cw_ref, wout_ref, o_ref):
    t = pl.program_id(0)
    tm = x_ref.shape[0]
    x = x_ref[...]
    m = mod_ref[...]
    shift, scale, gate = m[0:1], m[1:2], m[2:3]
    x_ext = jnp.concatenate([xp_ref[...], x, xn_ref[...]], axis=0)
    h = _norm_mod(x_ext, g_ref[...], scale, shift).astype(BF16)
    proj = _dot(h, win_ref[...])
    gate_b = proj[SUBLANES:SUBLANES + tm, :D]
    u_ext = proj[:, D:2 * D] * proj[:, 2 * D:]
    n_ext = tm + 2 * SUBLANES
    u = u_ext[SUBLANES:SUBLANES + tm]
    u_prev = pltpu.roll(u_ext, 1, axis=0)[SUBLANES:SUBLANES + tm]
    u_next = pltpu.roll(u_ext, n_ext - 1, axis=0)[SUBLANES:SUBLANES + tm]
    row = lax.broadcasted_iota(I32, (tm, 1), 0)
    grow = t * tm + row
    seq_mask = jnp.where(grow < T_CTX, SEQ - 1, DEC_SEQ - 1)
    pos = grow & seq_mask
    u_prev = jnp.where(pos == 0, 0.0, u_prev)
    u_next = jnp.where(pos == seq_mask, 0.0, u_next)
    cw = cw_ref[...]
    conv = u_prev * cw[0:1] + u * cw[1:2] + u_next * cw[2:3]
    y = (gate_b * conv).astype(BF16)
    o_ref[...] = x + gate * _dot(y, wout_ref[...])


def _conv_layer(x, mod_l, norm_g, w_in, conv_w, w_out):
    tm = TM_CONV
    n_tiles = T // tm
    sub_per_tile = tm // SUBLANES
    last_sub = T // SUBLANES - 1
    return pl.pallas_call(
        _conv_body,
        out_shape=jax.ShapeDtypeStruct((T, D), F32),
        grid=(n_tiles,),
        in_specs=[
            pl.BlockSpec((tm, D), lambda t: (t, 0)),
            pl.BlockSpec((SUBLANES, D), lambda t: (jnp.maximum(t * sub_per_tile - 1, 0), 0)),
            pl.BlockSpec((SUBLANES, D), lambda t: (jnp.minimum((t + 1) * sub_per_tile, last_sub), 0)),
            pl.BlockSpec((None, 6, D), lambda t: (_cond_row(t, tm), 0, 0)),
            pl.BlockSpec((1, D), lambda t: (0, 0)),
            pl.BlockSpec((D, 3 * D), lambda t: (0, 0)),
            pl.BlockSpec((3, D), lambda t: (0, 0)),
            pl.BlockSpec((D, D), lambda t: (0, 0)),
        ],
        out_specs=pl.BlockSpec((tm, D), lambda t: (t, 0)),
        compiler_params=_cparams(("arbitrary",)),
        name="conv_mixer",
    )(x, x, x, mod_l, norm_g.reshape(1, D), w_in, conv_w, w_out)


def _head_norm(blk, gain):
    lane = lax.broadcasted_iota(I32, blk.shape, 1)
    low = lane < HEAD_DIM
    s = blk * blk
    s_lo = jnp.sum(jnp.where(low, s, 0.0), axis=-1, keepdims=True)
    s_hi = jnp.sum(jnp.where(low, 0.0, s), axis=-1, keepdims=True)
    ms = jnp.where(low, s_lo, s_hi) * (1.0 / HEAD_DIM)
    return blk * lax.rsqrt(ms + EPS) * gain


def _rope(blk, cos, sin_signed):
    lane = lax.broadcasted_iota(I32, blk.shape, 1)
    first = (lane & 31) < 16
    partner = jnp.where(first, pltpu.roll(blk, LANES - 16, axis=1), pltpu.roll(blk, 16, axis=1))
    return blk * cos + partner * sin_signed


def _qkv_body(*refs, nq, nk, nv, rope, emit_f32, q_scale):
    refs = list(refs)
    x_ref, mod_ref, g_ref, w_ref, qg_ref, kg_ref = refs[:6]
    refs = refs[6:]
    if rope:
        cos_ref, sin_ref = refs[:2]
        refs = refs[2:]
    q_ref, k_ref, v_ref = refs[:3]
    refs = refs[3:]
    if emit_f32:
        kf_ref, vf_ref = refs[:2]
    x = x_ref[...]
    m = mod_ref[...]
    h = _norm_mod(x, g_ref[...], m[1:2], m[0:1]).astype(BF16)
    proj = _dot(h, w_ref[...])
    if rope:
        cos = cos_ref[...]
        sin = sin_ref[...]
    for j in range(nq):
        blk = _head_norm(proj[:, j * LANES:(j + 1) * LANES], qg_ref[...])
        if rope:
            blk = _rope(blk, cos, sin)
        q_ref[:, j * LANES:(j + 1) * LANES] = (blk * q_scale).astype(BF16)
    off = nq * LANES
    for j in range(nk):
        blk = _head_norm(proj[:, off + j * LANES:off + (j + 1) * LANES], kg_ref[...])
        if emit_f32:
            kf_ref[:, j * LANES:(j + 1) * LANES] = blk
        if rope:
            blk = _rope(blk, cos, sin)
        k_ref[:, j * LANES:(j + 1) * LANES] = blk.astype(BF16)
    off = (nq + nk) * LANES
    v = proj[:, off:off + nv]
    v_ref[...] = v.astype(BF16)
    if emit_f32:
        vf_ref[...] = v


def _qkv_proj(x, mod_l, norm_g, w, q_gain, k_gain, rope_tabs, *, latent, nq, nk, nv):
    tm = TM_QKV
    n_rows = T_LAT if latent else T_CTX
    n_tiles = n_rows // tm
    tile0 = (T_CTX // tm) if latent else 0
    n_proj = (nq + nk) * LANES + nv
    gain2 = lambda g: jnp.concatenate([g, g]).reshape(1, LANES)
    in_specs = [
        pl.BlockSpec((tm, D), lambda t: (t + tile0, 0)),
        pl.BlockSpec((None, 6, D), lambda t: (_cond_row(t + tile0, tm), 0, 0)),
        pl.BlockSpec((1, D), lambda t: (0, 0)),
        pl.BlockSpec((D, n_proj), lambda t: (0, 0)),
        pl.BlockSpec((1, LANES), lambda t: (0, 0)),
        pl.BlockSpec((1, LANES), lambda t: (0, 0)),
    ]
    args = [x, mod_l, norm_g.reshape(1, D), w, gain2(q_gain), gain2(k_gain)]
    if latent:
        per_seq = DEC_SEQ // tm
        in_specs += [pl.BlockSpec((tm, LANES), lambda t: (t % per_seq, 0))] * 2
        args += list(rope_tabs)
    out_shape = [
        jax.ShapeDtypeStruct((n_rows, nq * LANES), BF16),
        jax.ShapeDtypeStruct((n_rows, nk * LANES), BF16),
        jax.ShapeDtypeStruct((n_rows, nv), BF16),
    ]
    out_specs = [
        pl.BlockSpec((tm, nq * LANES), lambda t: (t, 0)),
        pl.BlockSpec((tm, nk * LANES), lambda t: (t, 0)),
        pl.BlockSpec((tm, nv), lambda t: (t, 0)),
    ]
    if not latent:
        out_shape += [jax.ShapeDtypeStruct((n_rows, nk * LANES), F32), jax.ShapeDtypeStruct((n_rows, nv), F32)]
        out_specs += [pl.BlockSpec((tm, nk * LANES), lambda t: (t, 0)), pl.BlockSpec((tm, nv), lambda t: (t, 0))]
    body = functools.partial(_qkv_body, nq=nq, nk=nk, nv=nv, rope=latent, emit_f32=not latent,
                             q_scale=HEAD_DIM ** -0.5)
    return pl.pallas_call(
        body,
        out_shape=out_shape,
        grid=(n_tiles,),
        in_specs=in_specs,
        out_specs=out_specs,
        compiler_params=_cparams(("arbitrary",)),
        name="qkv_latent" if latent else "qkv_context",
    )(*args)


def _rope_tables():
    rows = DEC_SEQ // GRID_W
    row_pos = jnp.repeat(jnp.arange(rows, dtype=F32), GRID_W)
    col_pos = jnp.tile(jnp.arange(GRID_W, dtype=F32), rows)
    half = HEAD_DIM // 2
    inv_freq = ROPE_BASE ** (-jnp.arange(0, half, 2, dtype=F32) / half)
    ang_r = row_pos[:, None] * inv_freq[None, :]
    ang_c = col_pos[:, None] * inv_freq[None, :]
    cr, sr, cc, sc = jnp.cos(ang_r), jnp.sin(ang_r), jnp.cos(ang_c), jnp.sin(ang_c)
    cos64 = jnp.concatenate([cr, cr, cc, cc], axis=-1)
    sin64 = jnp.concatenate([-sr, sr, -sc, sc], axis=-1)
    return jnp.concatenate([cos64, cos64], axis=-1), jnp.concatenate([sin64, sin64], axis=-1)


def _swap_halves(blk):
    return pltpu.roll(blk.astype(F32), HEAD_DIM, axis=1).astype(BF16)


def _kv_operands(kh, k_blk, v_blk):
    lane = lax.broadcasted_iota(I32, k_blk.shape, 1)
    low = lane < HEAD_DIM
    k_swp = _swap_halves(k_blk)
    if kh % 2 == 0:
        k2 = jnp.where(low, k_blk, k_swp)
        v_lo = v_blk
    else:
        k2 = jnp.where(low, k_swp, k_blk)
        v_lo = _swap_halves(v_blk)
    v2 = jnp.where(low, v_lo, jnp.ones_like(v_lo))
    return k2, v2


def _stack_query_heads(q_ref, kh, tq):
    lane = lax.broadcasted_iota(I32, (tq, LANES), 1)
    low = lane < HEAD_DIM
    parts = []
    for pair in range(SWA_GROUP // 2):
        blk = (SWA_GROUP // 2) * kh + pair
        qp = q_ref[:, blk * LANES:(blk + 1) * LANES]
        zero = jnp.zeros_like(qp)
        parts += [jnp.where(low, qp, zero), jnp.where(low, zero, qp)]
    return jnp.concatenate(parts, axis=0)


def _sink_column(sink_ref, kh, tq):
    row = lax.broadcasted_iota(I32, (SWA_GROUP * tq, 1), 0)
    col = jnp.full((SWA_GROUP * tq, 1), sink_ref[SWA_GROUP * kh], F32)
    for g in range(1, SWA_GROUP):
        col = jnp.where(row >= g * tq, sink_ref[SWA_GROUP * kh + g], col)
    return col


def _store_head_group(att_ref, kh, tq, acc, sink_term):
    lane = lax.broadcasted_iota(I32, (tq, LANES), 1)
    low = lane < HEAD_DIM
    den = acc[:, HEAD_DIM:HEAD_DIM + 1] + sink_term
    r = acc / den
    r_hi = pltpu.roll(r, HEAD_DIM, axis=1)
    for pair in range(SWA_GROUP // 2):
        blk = (SWA_GROUP // 2) * kh + pair
        a = r[(2 * pair) * tq:(2 * pair + 1) * tq]
        b = r_hi[(2 * pair + 1) * tq:(2 * pair + 2) * tq]
        att_ref[:, blk * LANES:(blk + 1) * LANES] = jnp.where(low, a, b).astype(BF16)


def _swa_ctx_body(sink_ref, x_ref, mod_ref, q_ref, k_ref, v_ref, wout_ref, o_ref, att_ref):
    tq = q_ref.shape[0]
    for kh in range(SWA_KV_HEADS):
        blk = kh // 2
        k2, v2 = _kv_operands(kh, k_ref[:, blk * LANES:(blk + 1) * LANES], v_ref[:, blk * LANES:(blk + 1) * LANES])
        qs = _stack_query_heads(q_ref, kh, tq)
        sink = _sink_column(sink_ref, kh, tq)
        s = _dot_nt(qs, k2)
        mx = jnp.maximum(jnp.max(s, axis=-1, keepdims=True), sink)
        p = jnp.exp(s - mx)
        acc = _dot(p.astype(BF16), v2)
        _store_head_group(att_ref, kh, tq, acc, jnp.exp(sink - mx))
    gate = mod_ref[2:3, :]
    o_ref[...] = x_ref[...] + gate * _dot(att_ref[...], wout_ref[...])


def _swa_ctx_attention(x, mod_l, q, k, v, sink, w_out):
    nkv = SWA_KV_HEADS * HEAD_DIM
    return pl.pallas_call(
        _swa_ctx_body,
        out_shape=jax.ShapeDtypeStruct((T, D), F32),
        grid=(BATCH,),
        in_specs=[
            pl.BlockSpec(memory_space=pltpu.SMEM),
            pl.BlockSpec((SEQ, D), lambda b: (b, 0)),
            pl.BlockSpec((None, 6, D), lambda b: (0, 0, 0)),
            pl.BlockSpec((SEQ, D), lambda b: (b, 0)),
            pl.BlockSpec((SEQ, nkv), lambda b: (b, 0)),
            pl.BlockSpec((SEQ, nkv), lambda b: (b, 0)),
            pl.BlockSpec((D, D), lambda b: (0, 0)),
        ],
        out_specs=pl.BlockSpec((SEQ, D), lambda b: (b, 0)),
        scratch_shapes=[pltpu.VMEM((SEQ, D), BF16)],
        input_output_aliases={1: 0},
        compiler_params=_cparams(("arbitrary",)),
        name="swa_context_attention",
    )(sink, x, mod_l, q, k, v, w_out)


SWA_TQ = 128
SWA_KWIN = 3 * SWA_TQ


def _swa_lat_body(sink_ref, x_ref, mod_ref, q_ref, k_ref, v_ref, kc_ref, vc_ref, wout_ref, o_ref, att_ref):
    j = pl.program_id(1)
    tq = SWA_TQ
    start = jnp.clip((j - 1) * tq, 0, DEC_SEQ - SWA_KWIN)
    start = pl.multiple_of(start, tq)
    row = lax.broadcasted_iota(I32, (SWA_GROUP * tq, 1), 0)
    q_pos = j * tq + (row & (tq - 1))
    k_pos = start + lax.broadcasted_iota(I32, (1, SWA_KWIN), 1)
    valid = jnp.abs(k_pos - q_pos) <= SWA_WINDOW
    for kh in range(SWA_KV_HEADS):
        blk = kh // 2
        lanes = slice(blk * LANES, (blk + 1) * LANES)
        k2, v2 = _kv_operands(kh, k_ref[pl.ds(start, SWA_KWIN), lanes], v_ref[pl.ds(start, SWA_KWIN), lanes])
        kc2, vc2 = _kv_operands(kh, kc_ref[:, lanes], vc_ref[:, lanes])
        qs = _stack_query_heads(q_ref, kh, tq)
        sink = _sink_column(sink_ref, kh, tq)
        s_lat = jnp.where(valid, _dot_nt(qs, k2), -jnp.inf)
        s_ctx = _dot_nt(qs, kc2)
        mx = jnp.maximum(jnp.max(s_lat, axis=-1, keepdims=True), jnp.max(s_ctx, axis=-1, keepdims=True))
        mx = jnp.maximum(mx, sink)
        p_lat = jnp.exp(s_lat - mx).astype(BF16)
        p_ctx = jnp.exp(s_ctx - mx).astype(BF16)
        acc = _dot(p_lat, v2) + _dot(p_ctx, vc2)
        _store_head_group(att_ref, kh, tq, acc, jnp.exp(sink - mx))
    gate = mod_ref[2:3, :]
    o_ref[...] = x_ref[...] + gate * _dot(att_ref[...], wout_ref[...])


def _swa_lat_attention(x, mod_l, q, k, v, k_cache, v_cache, sink, w_out):
    nkv = SWA_KV_HEADS * HEAD_DIM
    nb = DEC_SEQ // SWA_TQ
    x_tile0 = T_CTX // SWA_TQ
    return pl.pallas_call(
        _swa_lat_body,
        out_shape=jax.ShapeDtypeStruct((T, D), F32),
        grid=(DEC_BATCH, nb),
        in_specs=[
            pl.BlockSpec(memory_space=pltpu.SMEM),
            pl.BlockSpec((SWA_TQ, D), lambda b, j: (x_tile0 + b * nb + j, 0)),
            pl.BlockSpec((None, 6, D), lambda b, j: (1 + b, 0, 0)),
            pl.BlockSpec((SWA_TQ, D), lambda b, j: (b * nb + j, 0)),
            pl.BlockSpec((DEC_SEQ, nkv), lambda b, j: (b, 0)),
            pl.BlockSpec((DEC_SEQ, nkv), lambda b, j: (b, 0)),
            pl.BlockSpec((None, PAST_LEN, nkv), lambda b, j: (b, 0, 0)),
            pl.BlockSpec((None, PAST_LEN, nkv), lambda b, j: (b, 0, 0)),
            pl.BlockSpec((D, D), lambda b, j: (0, 0)),
        ],
        out_specs=pl.BlockSpec((SWA_TQ, D), lambda b, j: (x_tile0 + b * nb + j, 0)),
        scratch_shapes=[pltpu.VMEM((SWA_TQ, D), BF16)],
        input_output_aliases={1: 0},
        compiler_params=_cparams(("arbitrary", "arbitrary")),
        name="swa_latent_attention",
    )(sink, x, mod_l, q, k, v, k_cache, v_cache, w_out)


DIFF_TQ = 256


def _diff_body(x_ref, mod_ref, q_ref, k_ref, v_ref, lam_ref, sg_ref, wout_ref, o_ref, att_ref, *, lam_init):
    tq = q_ref.shape[0]
    lane = lax.broadcasted_iota(I32, (tq, LANES), 1)
    low = lane < HEAD_DIM
    lv = lam_ref[...]
    lam = (jnp.exp(jnp.sum(lv[0:1] * lv[1:2], axis=-1, keepdims=True))
           - jnp.exp(jnp.sum(lv[2:3] * lv[3:4], axis=-1, keepdims=True)) + lam_init)
    sub_gain = sg_ref[...]
    for hd in range(DIFF_HEADS):
        lanes = slice(hd * LANES, (hd + 1) * LANES)
        qh = q_ref[:, lanes]
        vh = v_ref[:, lanes]
        zero = jnp.zeros_like(qh)
        q2 = jnp.concatenate([jnp.where(low, qh, zero), jnp.where(low, zero, qh)], axis=0)
        s = _dot_nt(q2, k_ref[:, lanes])
        p = jnp.exp(s - jnp.max(s, axis=-1, keepdims=True)).astype(BF16)
        acc = _dot(p, jnp.concatenate([vh, jnp.ones_like(vh)], axis=1))
        r = acc[:, :LANES] / acc[:, LANES:LANES + 1]
        o = r[:tq] - lam * r[tq:]
        ms = jnp.mean(o * o, axis=-1, keepdims=True)
        o = (o * lax.rsqrt(ms + EPS) * sub_gain) * (1.0 - lam_init)
        att_ref[:, lanes] = o.astype(BF16)
    gate = mod_ref[2:3, :]
    o_ref[...] = x_ref[...] + gate * _dot(att_ref[...], wout_ref[...])


def _diff_attention(x, mod_l, q, k_all, v_all, lam_vecs, sub_gain, w_out, *, latent, lam_init):
    tq = DIFF_TQ
    n_keys = k_all.shape[1]
    if latent:
        per_batch = DEC_SEQ // tq
        grid = (DEC_BATCH, per_batch)
        x_tile0 = T_CTX // tq
        x_map = lambda b, j: (x_tile0 + b * per_batch + j, 0)
        q_map = lambda b, j: (b * per_batch + j, 0)
        mod_map = lambda b, j: (1 + b, 0, 0)
    else:
        per_batch = SEQ // tq
        grid = (BATCH, per_batch)
        x_map = lambda b, j: (b * per_batch + j, 0)
        q_map = x_map
        mod_map = lambda b, j: (0, 0, 0)
    const2 = lambda b, j: (0, 0)
    return pl.pallas_call(
        functools.partial(_diff_body, lam_init=lam_init),
        out_shape=jax.ShapeDtypeStruct((T, D), F32),
        grid=grid,
        in_specs=[
            pl.BlockSpec((tq, D), x_map),
            pl.BlockSpec((None, 6, D), mod_map),
            pl.BlockSpec((tq, D), q_map),
            pl.BlockSpec((None, n_keys, D), lambda b, j: (b, 0, 0)),
            pl.BlockSpec((None, n_keys, D), lambda b, j: (b, 0, 0)),
            pl.BlockSpec((4, HEAD_DIM), const2),
            pl.BlockSpec((1, 2 * HEAD_DIM), const2),
            pl.BlockSpec((D, D), const2),
        ],
        out_specs=pl.BlockSpec((tq, D), x_map),
        scratch_shapes=[pltpu.VMEM((tq, D), BF16)],
        input_output_aliases={0: 0},
        compiler_params=_cparams(("arbitrary", "arbitrary")),
        name="diff_latent_attention" if latent else "diff_context_attention",
    )(x, mod_l, q, k_all, v_all, lam_vecs, sub_gain, w_out)


def _dispatch_body(x_ref, mod_ref, g_ref, wrt_hi_ref, wrt_lo_ref, brt_ref, before_ref, below_ref,
                   z_ref, pg_ref, meta_ref):
    tm = TM_MOE
    m = mod_ref[...]
    h = _norm_mod(x_ref[...], g_ref[...], m[4:5], m[3:4])
    hb = h.astype(BF16)
    h_lo = (h - hb.astype(F32)).astype(BF16)
    wrt_hi = wrt_hi_ref[...]
    logits_t = _dot_nt(wrt_hi, hb) + _dot_nt(wrt_hi, h_lo) + _dot_nt(wrt_lo_ref[...], hb)
    work = logits_t[:N_EXPERTS] + brt_ref[...]
    sub = lax.broadcasted_iota(I32, (N_EXPERTS, tm), 0)
    picks, vals = [], []
    for _ in range(TOP_K):
        best = jnp.max(work, axis=0, keepdims=True)
        idx = jnp.min(jnp.where(work == best, sub, N_EXPERTS), axis=0, keepdims=True)
        hit = sub == idx
        work = jnp.where(hit, -jnp.inf, work)
        picks.append(hit)
        vals.append(best)
    exps = [jnp.exp(v - vals[0]) for v in vals]
    den = exps[0] + exps[1] + exps[2] + exps[3]
    gates = [e / den for e in exps]
    chosen_b = jnp.where(picks[0] | picks[1] | picks[2] | picks[3], 1.0, 0.0).astype(BF16)

    rank_t = _dot(chosen_b, before_ref[...])
    count_c = _dot(chosen_b, jnp.ones((tm, LANES), BF16))
    chunks_c = jnp.floor((count_c + (CHUNK - 1)) * (1.0 / CHUNK))
    chunks_pad = jnp.concatenate([chunks_c, jnp.zeros((LANES - N_EXPERTS, LANES), F32)], axis=0)
    first_c = _dot(below_ref[...], chunks_pad.astype(BF16))[:N_EXPERTS]
    pos_full_t = jnp.concatenate([first_c] * (tm // LANES), axis=1) * CHUNK + rank_t
    pos_rows = [jnp.sum(jnp.where(p, pos_full_t, 0.0), axis=0, keepdims=True) for p in picks]

    row_id = lax.broadcasted_iota(I32, (LOCAL_ROWS, tm), 0).astype(F32)
    sel_t = jnp.zeros((LOCAL_ROWS, tm), F32)
    for k in range(TOP_K):
        sel_t = jnp.where(row_id == pos_rows[k], 1.0, sel_t)
    rows = _dot(sel_t.astype(BF16), hb)
    z_ref[...] = _chunks_to_storage(rows)

    sub_l = lax.broadcasted_iota(I32, (LANES, tm), 0)
    pg_t = jnp.zeros((LANES, tm), F32)
    for k in range(TOP_K):
        pg_t = jnp.where(sub_l == k, pos_rows[k], pg_t)
        pg_t = jnp.where(sub_l == TOP_K + k, gates[k], pg_t)
    pg_ref[...] = pg_t.T

    diag = (lax.broadcasted_iota(I32, (N_EXPERTS, LANES), 0) == lax.broadcasted_iota(I32, (N_EXPERTS, LANES), 1))
    cnt_row = jnp.sum(jnp.where(diag, chunks_c, 0.0), axis=0, keepdims=True)
    first_row = jnp.sum(jnp.where(diag, first_c, 0.0), axis=0, keepdims=True)
    sub8 = lax.broadcasted_iota(I32, (SUBLANES, LANES), 0)
    meta = jnp.where(sub8 == 0, cnt_row, jnp.where(sub8 == 1, first_row, 0.0))
    meta_ref[0] = meta.astype(I32)


def _moe_dispatch(x, mod_l, norm_g, w_router, b_router):
    wrt = jnp.pad(w_router.T, ((0, LANES - N_EXPERTS), (0, 0)))
    wrt_hi = wrt.astype(BF16)
    wrt_lo = (wrt - wrt_hi.astype(F32)).astype(BF16)
    brt = b_router.reshape(N_EXPERTS, 1)
    r = jnp.arange(TM_MOE)
    before = (r[:, None] < r[None, :]).astype(BF16)
    c = jnp.arange(LANES)
    below = (c[None, :] < c[:, None]).astype(BF16)
    const2 = lambda t: (0, 0)
    return pl.pallas_call(
        _dispatch_body,
        out_shape=[
            jax.ShapeDtypeStruct((N_TILE_MOE * LOCAL_CHUNKS, 2 * CHUNK, HALF), BF16),
            jax.ShapeDtypeStruct((T, LANES), F32),
            jax.ShapeDtypeStruct((N_TILE_MOE, SUBLANES, LANES), I32),
        ],
        grid=(N_TILE_MOE,),
        in_specs=[
            pl.BlockSpec((TM_MOE, D), lambda t: (t, 0)),
            pl.BlockSpec((None, 6, D), lambda t: (_cond_row(t, TM_MOE), 0, 0)),
            pl.BlockSpec((1, D), const2),
            pl.BlockSpec((LANES, D), const2),
            pl.BlockSpec((LANES, D), const2),
            pl.BlockSpec((N_EXPERTS, 1), const2),
            pl.BlockSpec((TM_MOE, TM_MOE), const2),
            pl.BlockSpec((LANES, LANES), const2),
        ],
        out_specs=[
            pl.BlockSpec((LOCAL_CHUNKS, 2 * CHUNK, HALF), lambda t: (t, 0, 0)),
            pl.BlockSpec((TM_MOE, LANES), lambda t: (t, 0)),
            pl.BlockSpec((1, SUBLANES, LANES), lambda t: (t, 0, 0)),
        ],
        compiler_params=_cparams(("arbitrary",)),
        name="moe_dispatch",
    )(x, mod_l, norm_g.reshape(1, D), wrt_hi, wrt_lo, brt, before, below)


def _expert_body(cnt_ref, first_ref, tstart_ref, ntile_ref, z_hbm, wgu_ref, bgu_ref, wd_ref, bd_ref, perm_ref,
                 y_hbm, wgu_s, wd_s, xbuf, ybuf, src_tab, sem_in, sem_out):
    e = pl.program_id(0)
    t0 = tstart_ref[e]
    nt = ntile_ref[e]
    total = tstart_ref[N_EXPERTS - 1] + ntile_ref[N_EXPERTS - 1]

    def build_source_table():
        def clear(p, carry):
            src_tab[p] = 0
            return carry

        def per_expert(ex, carry):
            def per_token_tile(tt, pos):
                n = cnt_ref[tt * N_EXPERTS + ex]
                first = tt * LOCAL_CHUNKS + first_ref[tt * N_EXPERTS + ex]

                def per_chunk(c, inner):
                    src_tab[pos + c] = first + c
                    return inner

                lax.fori_loop(0, n, per_chunk, 0)
                return pos + n

            pos = lax.fori_loop(0, N_TILE_MOE, per_token_tile, tstart_ref[ex] * EXP_CHUNKS)
            lax.fori_loop(pos, (tstart_ref[ex] + ntile_ref[ex]) * EXP_CHUNKS, clear, 0)
            return carry

        lax.fori_loop(0, N_EXPERTS, per_expert, 0)

    def start_gather(g, slot):
        base = jnp.minimum(g, total - 1) * EXP_CHUNKS
        for c in range(EXP_CHUNKS):
            pltpu.make_async_copy(z_hbm.at[src_tab[base + c]], xbuf.at[slot, c], sem_in.at[slot]).start()

    def wait_gather(slot):
        pltpu.make_async_copy(z_hbm.at[pl.ds(0, EXP_CHUNKS)], xbuf.at[slot], sem_in.at[slot]).wait()

    def out_copy(g, slot):
        return pltpu.make_async_copy(ybuf.at[slot], y_hbm.at[pl.ds(g * EXP_CHUNKS, EXP_CHUNKS)],
                                     sem_out.at[slot])

    @pl.when(e == 0)
    def _():
        build_source_table()
        for g in range(N_XBUF - 1):
            start_gather(g, g)

    @pl.when(nt > 0)
    def _():
        for blk in range(2 * D_EXPERT // MXU_DIM):
            cols = slice(blk * MXU_DIM, (blk + 1) * MXU_DIM)
            wgu_s[:, cols] = _dot(wgu_ref[:, cols].astype(BF16), perm_ref[...]).astype(BF16)
        wd_s[...] = wd_ref[...].astype(BF16)

    def tile_body(i, carry):
        g = t0 + i
        slot = lax.rem(g, N_XBUF)
        oslot = lax.rem(g, 2)
        wait_gather(slot)
        x = _chunks_from_storage(xbuf[slot])
        acts = []
        for blk in range(2 * D_EXPERT // MXU_DIM):
            cols = slice(blk * MXU_DIM, (blk + 1) * MXU_DIM)
            gu = _dot(x, wgu_s[:, cols]) + bgu_ref[:, cols]
            x_glu = jnp.minimum(gu[:, :LANES], SWIGLU_LIMIT)
            x_lin = jnp.clip(gu[:, LANES:], -SWIGLU_LIMIT, SWIGLU_LIMIT)
            acts.append((x_glu * jax.nn.sigmoid(SWIGLU_ALPHA * x_glu) * (x_lin + 1.0)).astype(BF16))
        y = _dot(jnp.concatenate(acts, axis=1), wd_s[...]) + bd_ref[...]
        start_gather(g + N_XBUF - 1, lax.rem(g + N_XBUF - 1, N_XBUF))

        @pl.when(g >= 2)
        def _():
            out_copy(g - 2, oslot).wait()

        ybuf[oslot] = _chunks_to_storage(y)
        out_copy(g, oslot).start()
        return carry

    lax.fori_loop(0, nt, tile_body, 0)

    @pl.when(e == N_EXPERTS - 1)
    def _():
        for ahead in range(N_XBUF - 1):
            wait_gather(lax.rem(total + ahead, N_XBUF))
        last = total - 1
        out_copy(last, lax.rem(last, 2)).wait()

        @pl.when(total >= 2)
        def _():
            out_copy(last - 1, lax.rem(last - 1, 2)).wait()

        ybuf[0] = jnp.zeros((EXP_CHUNKS, 2 * CHUNK, HALF), BF16)

        def start_fill(g, carry):
            out_copy(g, 0).start()
            return carry

        def wait_fill(g, carry):
            out_copy(g, 0).wait()
            return carry

        lax.fori_loop(total, MAX_EXP_TILES, start_fill, 0)
        lax.fori_loop(total, MAX_EXP_TILES, wait_fill, 0)


def _moe_experts(z, cnt, first, tstart, ntile, layer, w_gate_up, bgu_all, w_down, b_down):
    n_gu = 2 * D_EXPERT
    i = jnp.arange(MXU_DIM)
    perm = (i[:, None] == jnp.where(i < LANES, 2 * i, 2 * (i - LANES) + 1)[None, :]).astype(BF16)
    wmap = lambda e, *_: (layer, e, 0, 0)
    grid_spec = pltpu.PrefetchScalarGridSpec(
        num_scalar_prefetch=4,
        grid=(N_EXPERTS,),
        in_specs=[
            pl.BlockSpec(memory_space=pl.ANY),
            pl.BlockSpec((None, None, D, n_gu), wmap),
            pl.BlockSpec((None, None, 1, n_gu), wmap),
            pl.BlockSpec((None, None, D_EXPERT, D), wmap),
            pl.BlockSpec((None, None, 1, D), wmap),
            pl.BlockSpec((MXU_DIM, MXU_DIM), lambda e, *_: (0, 0)),
        ],
        out_specs=pl.BlockSpec(memory_space=pl.ANY),
        scratch_shapes=[
            pltpu.VMEM((D, n_gu), BF16),
            pltpu.VMEM((D_EXPERT, D), BF16),
            pltpu.VMEM((N_XBUF, EXP_CHUNKS, 2 * CHUNK, HALF), BF16),
            pltpu.VMEM((2, EXP_CHUNKS, 2 * CHUNK, HALF), BF16),
            pltpu.SMEM((MAX_EXP_CHUNKS,), I32),
            pltpu.SemaphoreType.DMA((N_XBUF,)),
            pltpu.SemaphoreType.DMA((2,)),
        ],
    )
    return pl.pallas_call(
        _expert_body,
        out_shape=jax.ShapeDtypeStruct((MAX_EXP_CHUNKS, 2 * CHUNK, HALF), BF16),
        grid_spec=grid_spec,
        compiler_params=_cparams(("arbitrary",)),
        name="moe_experts",
    )(cnt, first, tstart, ntile, z, w_gate_up, bgu_all, w_down,
      b_down.reshape(DEPTH, N_EXPERTS, 1, D), perm)


def _combine_body(src_ref, x_ref, mod_ref, pg_ref, y_hbm, o_ref, ybuf, sem):
    t = pl.program_id(0)
    n_t = pl.num_programs(0)
    slot = lax.rem(t, N_YBUF)

    def start_gather(tile, sl):
        base = jnp.minimum(tile, n_t - 1) * LOCAL_CHUNKS
        for c in range(LOCAL_CHUNKS):
            pltpu.make_async_copy(y_hbm.at[src_ref[base + c]], ybuf.at[sl, c], sem.at[sl]).start()

    def wait_gather(sl):
        pltpu.make_async_copy(y_hbm.at[pl.ds(0, LOCAL_CHUNKS)], ybuf.at[sl], sem.at[sl]).wait()

    @pl.when(t == 0)
    def _():
        for ahead in range(N_YBUF - 1):
            start_gather(ahead, ahead)

    wait_gather(slot)
    y = _chunks_from_storage(ybuf[slot])
    pg = pg_ref[...]
    col = lax.broadcasted_iota(I32, (TM_MOE, LOCAL_ROWS), 1).astype(F32)
    comb = jnp.zeros((TM_MOE, LOCAL_ROWS), F32)
    for k in range(TOP_K):
        comb = jnp.where(col == pg[:, k:k + 1], pg[:, TOP_K + k:TOP_K + k + 1], comb)
    moe = _dot(comb.astype(BF16), y)
    ahead_tile = t + N_YBUF - 1
    start_gather(ahead_tile, lax.rem(ahead_tile, N_YBUF))
    o_ref[...] = x_ref[...] + mod_ref[5:6, :] * moe

    @pl.when(t == n_t - 1)
    def _():
        for ahead in range(1, N_YBUF):
            wait_gather(lax.rem(t + ahead, N_YBUF))


def _moe_combine(x, mod_l, pg, y, src):
    grid_spec = pltpu.PrefetchScalarGridSpec(
        num_scalar_prefetch=1,
        grid=(N_TILE_MOE,),
        in_specs=[
            pl.BlockSpec((TM_MOE, D), lambda t, *_: (t, 0)),
            pl.BlockSpec((None, 6, D), lambda t, *_: (_cond_row(t, TM_MOE), 0, 0)),
            pl.BlockSpec((TM_MOE, LANES), lambda t, *_: (t, 0)),
            pl.BlockSpec(memory_space=pl.ANY),
        ],
        out_specs=pl.BlockSpec((TM_MOE, D), lambda t, *_: (t, 0)),
        scratch_shapes=[
            pltpu.VMEM((N_YBUF, LOCAL_CHUNKS, 2 * CHUNK, HALF), BF16),
            pltpu.SemaphoreType.DMA((N_YBUF,)),
        ],
    )
    return pl.pallas_call(
        _combine_body,
        out_shape=jax.ShapeDtypeStruct((T, D), F32),
        grid_spec=grid_spec,
        input_output_aliases={1: 0},
        compiler_params=_cparams(("arbitrary",)),
        name="moe_combine",
    )(src, x, mod_l, pg, y)


def _moe_layer(x, mod_l, norm_g, layer, w_router, b_router, w_gate_up, bgu_all, w_down, b_down):
    z, pg, meta = _moe_dispatch(x, mod_l, norm_g, w_router, b_router)
    cnt = meta[:, 0, :N_EXPERTS]
    first = meta[:, 1, :N_EXPERTS]
    ntile = (jnp.sum(cnt, axis=0) + EXP_CHUNKS - 1) // EXP_CHUNKS
    tstart = jnp.cumsum(ntile) - ntile
    y = _moe_experts(z, cnt.reshape(-1), first.reshape(-1), tstart, ntile, layer,
                     w_gate_up, bgu_all, w_down, b_down)
    end = first + cnt
    seen = jnp.cumsum(cnt, axis=0) - cnt
    j = jnp.arange(LOCAL_CHUNKS, dtype=I32)
    e_of_j = jnp.sum((end[:, None, :] <= j[None, :, None]).astype(I32), axis=-1)
    used = j[None, :] < end[:, -1:]
    own = jnp.arange(N_EXPERTS, dtype=I32)[None, None, :] == e_of_j[:, :, None]
    offset = tstart[None, :] * EXP_CHUNKS + seen - first
    src = j[None, :] + jnp.sum(jnp.where(own, offset[:, None, :], 0), axis=-1)
    src = jnp.where(used, src, 0).astype(I32).reshape(-1)
    return _moe_combine(x, mod_l, pg, y, src)


def kernel(x_prompt, x_sample, c, cache_swa_k, cache_swa_v, cache_diff_k, cache_diff_v, c_ctx, norm_mix_g, norm_ffn_g, ada_w, ada_b, conv_w_in, conv_w, conv_w_out, swa_w_qkv, swa_q_gain, swa_k_gain, swa_sink, swa_w_out, diff_w_qkv, diff_q_gain, diff_k_gain, diff_lam_q1, diff_lam_k1, diff_lam_q2, diff_lam_k2, diff_sub_gain, diff_w_out, moe_w_router, moe_b_router, moe_w_gate_up, moe_b_gate_up, moe_w_down, moe_b_down):
    x = jnp.concatenate([x_prompt.reshape(T_CTX, D), x_sample.reshape(T_LAT, D)], axis=0)
    cond = jnp.concatenate([c_ctx[None, :], c, jnp.zeros((COND_PAD - N_COND, D), F32)], axis=0)
    mod = _modulation(cond, ada_w, ada_b).reshape(DEPTH, COND_PAD, 6, D)
    rope_tabs = _rope_tables()
    n_gu = 2 * D_EXPERT
    bgu_all = moe_b_gate_up.reshape(DEPTH, N_EXPERTS, n_gu // MXU_DIM, LANES, 2)
    bgu_all = bgu_all.transpose(0, 1, 2, 4, 3).reshape(DEPTH, N_EXPERTS, 1, n_gu)
    new_kv = {}
    for i in range(DEPTH):
        j = i // 3
        kind = i % 3
        mod_l = mod[i]
        if kind == 0:
            x = _conv_layer(x, mod_l, norm_mix_g[i], conv_w_in[j].astype(BF16), conv_w[j],
                            conv_w_out[j].astype(BF16))
        elif kind == 1:
            w = swa_w_qkv[j].astype(BF16)
            w_out = swa_w_out[j].astype(BF16)
            nkv = SWA_KV_HEADS * HEAD_DIM
            dims = dict(nq=SWA_Q_HEADS * HEAD_DIM // LANES, nk=nkv // LANES, nv=nkv)
            q, k, v, kf, vf = _qkv_proj(x, mod_l, norm_mix_g[i], w, swa_q_gain[j], swa_k_gain[j], None,
                                        latent=False, **dims)
            new_kv["swa_k"] = kf.reshape(BATCH, 1, SEQ, SWA_KV_HEADS, HEAD_DIM)
            new_kv["swa_v"] = vf.reshape(BATCH, 1, SEQ, SWA_KV_HEADS, HEAD_DIM)
            x = _swa_ctx_attention(x, mod_l, q, k, v, swa_sink[j], w_out)
            q, k, v = _qkv_proj(x, mod_l, norm_mix_g[i], w, swa_q_gain[j], swa_k_gain[j], rope_tabs,
                                latent=True, **dims)
            kc = cache_swa_k[:, j].reshape(DEC_BATCH, PAST_LEN, nkv).astype(BF16)
            vc = cache_swa_v[:, j].reshape(DEC_BATCH, PAST_LEN, nkv).astype(BF16)
            x = _swa_lat_attention(x, mod_l, q, k, v, kc, vc, swa_sink[j], w_out)
        else:
            w = diff_w_qkv[j].astype(BF16)
            w_out = diff_w_out[j].astype(BF16)
            lam_init = 0.8 - 0.6 * math.exp(-0.3 * i)
            lam_vecs = jnp.stack([diff_lam_q1[j], diff_lam_k1[j], diff_lam_q2[j], diff_lam_k2[j]])
            sub_gain = diff_sub_gain[j].reshape(1, 2 * HEAD_DIM)
            dims = dict(nq=D // LANES, nk=D // LANES, nv=D)
            q, k, v, kf, vf = _qkv_proj(x, mod_l, norm_mix_g[i], w, diff_q_gain[j], diff_k_gain[j], None,
                                        latent=False, **dims)
            new_kv["diff_k"] = kf.reshape(BATCH, 1, SEQ, DIFF_HEADS, 2, HEAD_DIM)
            new_kv["diff_v"] = vf.reshape(BATCH, 1, SEQ, DIFF_HEADS, 2 * HEAD_DIM)
            x = _diff_attention(x, mod_l, q, k.reshape(BATCH, SEQ, D), v.reshape(BATCH, SEQ, D), lam_vecs,
                                sub_gain, w_out, latent=False, lam_init=lam_init)
            q, k, v = _qkv_proj(x, mod_l, norm_mix_g[i], w, diff_q_gain[j], diff_k_gain[j], rope_tabs,
                                latent=True, **dims)
            kc = cache_diff_k[:, j].reshape(DEC_BATCH, PAST_LEN, D).astype(BF16)
            vc = cache_diff_v[:, j].reshape(DEC_BATCH, PAST_LEN, D).astype(BF16)
            k_all = jnp.concatenate([k.reshape(DEC_BATCH, DEC_SEQ, D), kc], axis=1)
            v_all = jnp.concatenate([v.reshape(DEC_BATCH, DEC_SEQ, D), vc], axis=1)
            x = _diff_attention(x, mod_l, q, k_all, v_all, lam_vecs, sub_gain, w_out, latent=True,
                                lam_init=lam_init)
        x = _moe_layer(x, mod_l, norm_ffn_g[i], i, moe_w_router[i], moe_b_router[i], moe_w_gate_up,
                       bgu_all, moe_w_down, moe_b_down)
    y_prompt = x[:T_CTX].reshape(BATCH, SEQ, D)
    y_sample = x[T_CTX:].reshape(DEC_BATCH, DEC_SEQ, D)
    return (y_prompt, y_sample, new_kv["swa_k"], new_kv["swa_v"], new_kv["diff_k"], new_kv["diff_v"])
```

```python
import functools
import math

import jax
import jax.numpy as jnp
from jax import lax
from jax.experimental import pallas as pl
from jax.experimental.pallas import tpu as pltpu

F32 = jnp.float32
BF16 = jnp.bfloat16
I32 = jnp.int32

D = 1024
BATCH = 32
SEQ = 256
DEC_BATCH = 4
DEC_SEQ = 2048
PAST_LEN = 512
DEPTH = 4
GRID_W = 64
HEAD_DIM = 64
SWA_Q_HEADS = 16
SWA_KV_HEADS = 4
SWA_GROUP = SWA_Q_HEADS // SWA_KV_HEADS
SWA_WINDOW = 128
DIFF_HEADS = 8
N_EXPERTS = 32
TOP_K = 4
D_EXPERT = 1024
SWIGLU_ALPHA = 1.702
SWIGLU_LIMIT = 7.0
ROPE_BASE = 10000.0
EPS = 1e-6

T_CTX = BATCH * SEQ
T_LAT = DEC_BATCH * DEC_SEQ
T = T_CTX + T_LAT
N_COND = 1 + DEC_BATCH
COND_PAD = 8

LANES = 128
SUBLANES = 8
BF16_ROWS = 16
MXU_DIM = 256

TM_CONV = 512
TM_QKV = 512
TM_MOE = 256
TM_EXP = 256
CHUNK = SUBLANES
HALF = D // 2
assert 2 * CHUNK == BF16_ROWS
N_TILE_MOE = T // TM_MOE
LOCAL_ROWS = TM_MOE * TOP_K + N_EXPERTS * (CHUNK - 1)
LOCAL_ROWS = -(-LOCAL_ROWS // LANES) * LANES
LOCAL_CHUNKS = LOCAL_ROWS // CHUNK
EXP_CHUNKS = TM_EXP // CHUNK
MAX_GROUP_ROWS = T * TOP_K + N_TILE_MOE * N_EXPERTS * (CHUNK - 1)
MAX_EXP_TILES = -(-MAX_GROUP_ROWS // TM_EXP) + N_EXPERTS
MAX_EXP_CHUNKS = MAX_EXP_TILES * EXP_CHUNKS
N_XBUF = 3
N_YBUF = 3


def _chunks_to_storage(rows):
    r3 = rows.reshape(rows.shape[0] // CHUNK, CHUNK, D)
    return jnp.concatenate([r3[:, :, :HALF], r3[:, :, HALF:]], axis=1).astype(BF16)


def _chunks_from_storage(stored):
    s = stored.astype(F32)
    r3 = jnp.concatenate([s[:, :CHUNK, :], s[:, CHUNK:, :]], axis=-1)
    return r3.reshape(stored.shape[0] * CHUNK, D).astype(BF16)

VMEM_LIMIT = 56 * 1024 * 1024


def _cparams(sem, vmem=VMEM_LIMIT):
    return pltpu.CompilerParams(dimension_semantics=sem, vmem_limit_bytes=vmem)


def _dot(a, b):
    return jnp.dot(a, b, preferred_element_type=F32)


def _dot_nt(a, b):
    return lax.dot_general(a, b, (((1,), (1,)), ((), ())), preferred_element_type=F32)


def _dot_tn(a, b):
    return lax.dot_general(a, b, (((0,), (0,)), ((), ())), preferred_element_type=F32)


def _norm_mod(x, g, scale, shift):
    ms = jnp.mean(x * x, axis=-1, keepdims=True)
    y = x * lax.rsqrt(ms + EPS)
    return (y * g) * (1.0 + scale) + shift


MOD_TN = 1536


def _mod_body(cond_ref, w_ref, b_ref, o_ref):
    c = cond_ref[...]
    s = (c * jax.nn.sigmoid(c)).astype(BF16)
    o_ref[0] = _dot(s, w_ref[0].astype(BF16)) + b_ref[0]


def _modulation(cond, ada_w, ada_b):
    n_layer = ada_w.shape[0]
    n_out = ada_w.shape[2]
    return pl.pallas_call(
        _mod_body,
        out_shape=jax.ShapeDtypeStruct((n_layer, COND_PAD, n_out), F32),
        grid=(n_layer, n_out // MOD_TN),
        in_specs=[
            pl.BlockSpec((COND_PAD, D), lambda i, j: (0, 0)),
            pl.BlockSpec((1, D, MOD_TN), lambda i, j: (i, 0, j)),
            pl.BlockSpec((1, 1, MOD_TN), lambda i, j: (i, 0, j)),
        ],
        out_specs=pl.BlockSpec((1, COND_PAD, MOD_TN), lambda i, j: (i, 0, j)),
        compiler_params=_cparams(("arbitrary", "arbitrary")),
        name="modulation",
    )(cond, ada_w, ada_b.reshape(n_layer, 1, n_out))


def _cond_row(row_tile, tile_rows):
    ctx_tiles = T_CTX // tile_rows
    per_batch = DEC_SEQ // tile_rows
    return jnp.where(row_tile < ctx_tiles, 0, 1 + (row_tile - ctx_tiles) // per_batch)


def _conv_body(*refs, split):
    t = pl.program_id(0)
    if split:
        xc_ref, xl_ref, xcp_ref, xlp_ref, xcn_ref, xln_ref = refs[:6]
        refs = refs[6:]
        is_ctx = t < T_CTX // xc_ref.shape[0]
        x = jnp.where(is_ctx, xc_ref[...], xl_ref[...])
        x_before = jnp.where(is_ctx, xcp_ref[...], xlp_ref[...])
        x_after = jnp.where(is_ctx, xcn_ref[...], xln_ref[...])
    else:
        x_ref, xp_ref, xn_ref = refs[:3]
        refs = refs[3:]
        x, x_before, x_after = x_ref[...], xp_ref[...], xn_ref[...]
    mod_ref, g_ref, win_ref, cw_ref, wout_ref, o_ref = refs
    tm = x.shape[0]
    m = mod_ref[...]
    shift, scale, gate = m[0:1], m[1:2], m[2:3]
    x_ext = jnp.concatenate([x_before, x, x_after], axis=0)
    h = _norm_mod(x_ext, g_ref[...], scale, shift).astype(BF16)
    proj = _dot(h, win_ref[...])
    gate_b = proj[SUBLANES:SUBLANES + tm, :D]
    u_ext = proj[:, D:2 * D] * proj[:, 2 * D:]
    n_ext = tm + 2 * SUBLANES
    u = u_ext[SUBLANES:SUBLANES + tm]
    u_prev = pltpu.roll(u_ext, 1, axis=0)[SUBLANES:SUBLANES + tm]
    u_next = pltpu.roll(u_ext, n_ext - 1, axis=0)[SUBLANES:SUBLANES + tm]
    row = lax.broadcasted_iota(I32, (tm, 1), 0)
    grow = t * tm + row
    seq_mask = jnp.where(grow < T_CTX, SEQ - 1, DEC_SEQ - 1)
    pos = grow & seq_mask
    u_prev = jnp.where(pos == 0, 0.0, u_prev)
    u_next = jnp.where(pos == seq_mask, 0.0, u_next)
    cw = cw_ref[...]
    conv = u_prev * cw[0:1] + u * cw[1:2] + u_next * cw[2:3]
    y = (gate_b * conv).astype(BF16)
    o_ref[...] = x + gate * _dot(y, wout_ref[...])


def _conv_layer(xs, mod_l, norm_g, w_in, conv_w, w_out):
    tm = TM_CONV
    n_tiles = T // tm
    sub_per_tile = tm // SUBLANES
    split = isinstance(xs, tuple)
    if split:
        x_specs, x_args = [], []
        for kind in ("tile", "before", "after"):
            for x_part, tile0 in zip(xs, (0, T_CTX // tm)):
                part_tiles = x_part.shape[0] // tm
                last_sub = x_part.shape[0] // SUBLANES - 1
                if kind == "tile":
                    spec = pl.BlockSpec((tm, D), functools.partial(
                        lambda t, tile0, part_tiles: (jnp.clip(t - tile0, 0, part_tiles - 1), 0),
                        tile0=tile0, part_tiles=part_tiles))
                else:
                    shift = -1 if kind == "before" else sub_per_tile
                    spec = pl.BlockSpec((SUBLANES, D), functools.partial(
                        lambda t, tile0, last_sub, shift: (jnp.clip((t - tile0) * sub_per_tile + shift, 0, last_sub), 0),
                        tile0=tile0, last_sub=last_sub, shift=shift))
                x_specs.append(spec)
                x_args.append(x_part)
    else:
        last_sub = T // SUBLANES - 1
        x_specs = [
            pl.BlockSpec((tm, D), lambda t: (t, 0)),
            pl.BlockSpec((SUBLANES, D), lambda t: (jnp.maximum(t * sub_per_tile - 1, 0), 0)),
            pl.BlockSpec((SUBLANES, D), lambda t: (jnp.minimum((t + 1) * sub_per_tile, last_sub), 0)),
        ]
        x_args = [xs, xs, xs]
    return pl.pallas_call(
        functools.partial(_conv_body, split=split),
        out_shape=jax.ShapeDtypeStruct((T, D), F32),
        grid=(n_tiles,),
        in_specs=x_specs + [
            pl.BlockSpec((None, 6, D), lambda t: (_cond_row(t, tm), 0, 0)),
            pl.BlockSpec((1, D), lambda t: (0, 0)),
            pl.BlockSpec((D, 3 * D), lambda t: (0, 0)),
            pl.BlockSpec((3, D), lambda t: (0, 0)),
            pl.BlockSpec((D, D), lambda t: (0, 0)),
        ],
        out_specs=pl.BlockSpec((tm, D), lambda t: (t, 0)),
        compiler_params=_cparams(("arbitrary",)),
        name="conv_mixer",
    )(*x_args, mod_l, norm_g.reshape(1, D), w_in, conv_w, w_out)


def _head_norm(blk, gain):
    lane = lax.broadcasted_iota(I32, blk.shape, 1)
    low = lane < HEAD_DIM
    s = blk * blk
    s_lo = jnp.sum(jnp.where(low, s, 0.0), axis=-1, keepdims=True)
    s_hi = jnp.sum(jnp.where(low, 0.0, s), axis=-1, keepdims=True)
    ms = jnp.where(low, s_lo, s_hi) * (1.0 / HEAD_DIM)
    return blk * lax.rsqrt(ms + EPS) * gain


def _rope(blk, cos, sin_signed):
    lane = lax.broadcasted_iota(I32, blk.shape, 1)
    first = (lane & 31) < 16
    partner = jnp.where(first, pltpu.roll(blk, LANES - 16, axis=1), pltpu.roll(blk, 16, axis=1))
    return blk * cos + partner * sin_signed


def _qkv_body(*refs, nq, nk, nv, rope, emit_f32, q_scale):
    refs = list(refs)
    x_ref, mod_ref, g_ref, w_ref, qg_ref, kg_ref = refs[:6]
    refs = refs[6:]
    if rope:
        cos_ref, sin_ref = refs[:2]
        refs = refs[2:]
    q_ref, k_ref, v_ref = refs[:3]
    refs = refs[3:]
    if emit_f32:
        kf_ref, vf_ref = refs[:2]
    x = x_ref[...]
    m = mod_ref[...]
    h = _norm_mod(x, g_ref[...], m[1:2], m[0:1]).astype(BF16)
    proj = _dot(h, w_ref[...])
    if rope:
        cos = cos_ref[...]
        sin = sin_ref[...]
    for j in range(nq):
        blk = _head_norm(proj[:, j * LANES:(j + 1) * LANES], qg_ref[...])
        if rope:
            blk = _rope(blk, cos, sin)
        q_ref[:, j * LANES:(j + 1) * LANES] = (blk * q_scale).astype(BF16)
    off = nq * LANES
    for j in range(nk):
        blk = _head_norm(proj[:, off + j * LANES:off + (j + 1) * LANES], kg_ref[...])
        if emit_f32:
            kf_ref[:, j * LANES:(j + 1) * LANES] = blk
        if rope:
            blk = _rope(blk, cos, sin)
        k_ref[:, j * LANES:(j + 1) * LANES] = blk.astype(BF16)
    off = (nq + nk) * LANES
    v = proj[:, off:off + nv]
    v_ref[...] = v.astype(BF16)
    if emit_f32:
        vf_ref[...] = v


def _qkv_proj(x, mod_l, norm_g, w, q_gain, k_gain, rope_tabs, *, latent, nq, nk, nv):
    tm = TM_QKV
    n_rows = T_LAT if latent else T_CTX
    n_tiles = n_rows // tm
    tile0 = (T_CTX // tm) if latent else 0
    n_proj = (nq + nk) * LANES + nv
    gain2 = lambda g: jnp.concatenate([g, g]).reshape(1, LANES)
    in_specs = [
        pl.BlockSpec((tm, D), lambda t: (t + tile0, 0)),
        pl.BlockSpec((None, 6, D), lambda t: (_cond_row(t + tile0, tm), 0, 0)),
        pl.BlockSpec((1, D), lambda t: (0, 0)),
        pl.BlockSpec((D, n_proj), lambda t: (0, 0)),
        pl.BlockSpec((1, LANES), lambda t: (0, 0)),
        pl.BlockSpec((1, LANES), lambda t: (0, 0)),
    ]
    args = [x, mod_l, norm_g.reshape(1, D), w, gain2(q_gain), gain2(k_gain)]
    if latent:
        per_seq = DEC_SEQ // tm
        in_specs += [pl.BlockSpec((tm, LANES), lambda t: (t % per_seq, 0))] * 2
        args += list(rope_tabs)
    out_shape = [
        jax.ShapeDtypeStruct((n_rows, nq * LANES), BF16),
        jax.ShapeDtypeStruct((n_rows, nk * LANES), BF16),
        jax.ShapeDtypeStruct((n_rows, nv), BF16),
    ]
    out_specs = [
        pl.BlockSpec((tm, nq * LANES), lambda t: (t, 0)),
        pl.BlockSpec((tm, nk * LANES), lambda t: (t, 0)),
        pl.BlockSpec((tm, nv), lambda t: (t, 0)),
    ]
    if not latent:
        out_shape += [jax.ShapeDtypeStruct((n_rows, nk * LANES), F32), jax.ShapeDtypeStruct((n_rows, nv), F32)]
        out_specs += [pl.BlockSpec((tm, nk * LANES), lambda t: (t, 0)), pl.BlockSpec((tm, nv), lambda t: (t, 0))]
    body = functools.partial(_qkv_body, nq=nq, nk=nk, nv=nv, rope=latent, emit_f32=not latent,
                             q_scale=HEAD_DIM ** -0.5)
    return pl.pallas_call(
        body,
        out_shape=out_shape,
        grid=(n_tiles,),
        in_specs=in_specs,
        out_specs=out_specs,
        compiler_params=_cparams(("arbitrary",)),
        name="qkv_latent" if latent else "qkv_context",
    )(*args)


def _rope_tables():
    rows = DEC_SEQ // GRID_W
    row_pos = jnp.repeat(jnp.arange(rows, dtype=F32), GRID_W)
    col_pos = jnp.tile(jnp.arange(GRID_W, dtype=F32), rows)
    half = HEAD_DIM // 2
    inv_freq = ROPE_BASE ** (-jnp.arange(0, half, 2, dtype=F32) / half)
    ang_r = row_pos[:, None] * inv_freq[None, :]
    ang_c = col_pos[:, None] * inv_freq[None, :]
    cr, sr, cc, sc = jnp.cos(ang_r), jnp.sin(ang_r), jnp.cos(ang_c), jnp.sin(ang_c)
    cos64 = jnp.concatenate([cr, cr, cc, cc], axis=-1)
    sin64 = jnp.concatenate([-sr, sr, -sc, sc], axis=-1)
    return jnp.concatenate([cos64, cos64], axis=-1), jnp.concatenate([sin64, sin64], axis=-1)


def _swap_halves(blk):
    return pltpu.roll(blk.astype(F32), HEAD_DIM, axis=1).astype(BF16)


def _kv_operands(kh, k_blk, v_blk):
    lane = lax.broadcasted_iota(I32, k_blk.shape, 1)
    low = lane < HEAD_DIM
    k_swp = _swap_halves(k_blk)
    if kh % 2 == 0:
        k2 = jnp.where(low, k_blk, k_swp)
        v_lo = v_blk
    else:
        k2 = jnp.where(low, k_swp, k_blk)
        v_lo = _swap_halves(v_blk)
    v2 = jnp.where(low, v_lo, jnp.ones_like(v_lo))
    return k2, v2


def _stack_query_heads(q_ref, kh, tq):
    lane = lax.broadcasted_iota(I32, (tq, LANES), 1)
    low = lane < HEAD_DIM
    parts = []
    for pair in range(SWA_GROUP // 2):
        blk = (SWA_GROUP // 2) * kh + pair
        qp = q_ref[:, blk * LANES:(blk + 1) * LANES]
        zero = jnp.zeros_like(qp)
        parts += [jnp.where(low, qp, zero), jnp.where(low, zero, qp)]
    return jnp.concatenate(parts, axis=0)


def _sink_column(sink_ref, kh, tq):
    row = lax.broadcasted_iota(I32, (SWA_GROUP * tq, 1), 0)
    col = jnp.full((SWA_GROUP * tq, 1), sink_ref[SWA_GROUP * kh], F32)
    for g in range(1, SWA_GROUP):
        col = jnp.where(row >= g * tq, sink_ref[SWA_GROUP * kh + g], col)
    return col


def _store_head_group(att_ref, kh, tq, acc, sink_term):
    lane = lax.broadcasted_iota(I32, (tq, LANES), 1)
    low = lane < HEAD_DIM
    den = acc[:, HEAD_DIM:HEAD_DIM + 1] + sink_term
    r = acc / den
    r_hi = pltpu.roll(r, HEAD_DIM, axis=1)
    for pair in range(SWA_GROUP // 2):
        blk = (SWA_GROUP // 2) * kh + pair
        a = r[(2 * pair) * tq:(2 * pair + 1) * tq]
        b = r_hi[(2 * pair + 1) * tq:(2 * pair + 2) * tq]
        att_ref[:, blk * LANES:(blk + 1) * LANES] = jnp.where(low, a, b).astype(BF16)


def _swa_ctx_body(sink_ref, x_ref, mod_ref, q_ref, k_ref, v_ref, wout_ref, o_ref, att_ref):
    tq = q_ref.shape[0]
    for kh in range(SWA_KV_HEADS):
        blk = kh // 2
        k2, v2 = _kv_operands(kh, k_ref[:, blk * LANES:(blk + 1) * LANES], v_ref[:, blk * LANES:(blk + 1) * LANES])
        qs = _stack_query_heads(q_ref, kh, tq)
        sink = _sink_column(sink_ref, kh, tq)
        s = _dot_nt(qs, k2)
        mx = jnp.maximum(jnp.max(s, axis=-1, keepdims=True), sink)
        p = jnp.exp(s - mx)
        acc = _dot(p.astype(BF16), v2)
        _store_head_group(att_ref, kh, tq, acc, jnp.exp(sink - mx))
    gate = mod_ref[2:3, :]
    o_ref[...] = x_ref[...] + gate * _dot(att_ref[...], wout_ref[...])


def _swa_ctx_attention(x, mod_l, q, k, v, sink, w_out):
    nkv = SWA_KV_HEADS * HEAD_DIM
    return pl.pallas_call(
        _swa_ctx_body,
        out_shape=jax.ShapeDtypeStruct((T, D), F32),
        grid=(BATCH,),
        in_specs=[
            pl.BlockSpec(memory_space=pltpu.SMEM),
            pl.BlockSpec((SEQ, D), lambda b: (b, 0)),
            pl.BlockSpec((None, 6, D), lambda b: (0, 0, 0)),
            pl.BlockSpec((SEQ, D), lambda b: (b, 0)),
            pl.BlockSpec((SEQ, nkv), lambda b: (b, 0)),
            pl.BlockSpec((SEQ, nkv), lambda b: (b, 0)),
            pl.BlockSpec((D, D), lambda b: (0, 0)),
        ],
        out_specs=pl.BlockSpec((SEQ, D), lambda b: (b, 0)),
        scratch_shapes=[pltpu.VMEM((SEQ, D), BF16)],
        input_output_aliases={1: 0},
        compiler_params=_cparams(("arbitrary",)),
        name="swa_context_attention",
    )(sink, x, mod_l, q, k, v, w_out)


SWA_TQ = 128
SWA_KWIN = 3 * SWA_TQ


def _swa_lat_body(sink_ref, x_ref, mod_ref, q_ref, k_ref, v_ref, kc_ref, vc_ref, wout_ref, o_ref, att_ref):
    j = pl.program_id(1)
    tq = SWA_TQ
    start = jnp.clip((j - 1) * tq, 0, DEC_SEQ - SWA_KWIN)
    start = pl.multiple_of(start, tq)
    row = lax.broadcasted_iota(I32, (SWA_GROUP * tq, 1), 0)
    q_pos = j * tq + (row & (tq - 1))
    k_pos = start + lax.broadcasted_iota(I32, (1, SWA_KWIN), 1)
    valid = jnp.abs(k_pos - q_pos) <= SWA_WINDOW
    for kh in range(SWA_KV_HEADS):
        blk = kh // 2
        lanes = slice(blk * LANES, (blk + 1) * LANES)
        k2, v2 = _kv_operands(kh, k_ref[pl.ds(start, SWA_KWIN), lanes], v_ref[pl.ds(start, SWA_KWIN), lanes])
        kc2, vc2 = _kv_operands(kh, kc_ref[:, lanes], vc_ref[:, lanes])
        qs = _stack_query_heads(q_ref, kh, tq)
        sink = _sink_column(sink_ref, kh, tq)
        s_lat = jnp.where(valid, _dot_nt(qs, k2), -jnp.inf)
        s_ctx = _dot_nt(qs, kc2)
        mx = jnp.maximum(jnp.max(s_lat, axis=-1, keepdims=True), jnp.max(s_ctx, axis=-1, keepdims=True))
        mx = jnp.maximum(mx, sink)
        p_lat = jnp.exp(s_lat - mx).astype(BF16)
        p_ctx = jnp.exp(s_ctx - mx).astype(BF16)
        acc = _dot(p_lat, v2) + _dot(p_ctx, vc2)
        _store_head_group(att_ref, kh, tq, acc, jnp.exp(sink - mx))
    gate = mod_ref[2:3, :]
    o_ref[...] = x_ref[...] + gate * _dot(att_ref[...], wout_ref[...])


def _swa_lat_attention(x, mod_l, q, k, v, k_cache, v_cache, sink, w_out):
    nkv = SWA_KV_HEADS * HEAD_DIM
    nb = DEC_SEQ // SWA_TQ
    x_tile0 = T_CTX // SWA_TQ
    return pl.pallas_call(
        _swa_lat_body,
        out_shape=jax.ShapeDtypeStruct((T, D), F32),
        grid=(DEC_BATCH, nb),
        in_specs=[
            pl.BlockSpec(memory_space=pltpu.SMEM),
            pl.BlockSpec((SWA_TQ, D), lambda b, j: (x_tile0 + b * nb + j, 0)),
            pl.BlockSpec((None, 6, D), lambda b, j: (1 + b, 0, 0)),
            pl.BlockSpec((SWA_TQ, D), lambda b, j: (b * nb + j, 0)),
            pl.BlockSpec((DEC_SEQ, nkv), lambda b, j: (b, 0)),
            pl.BlockSpec((DEC_SEQ, nkv), lambda b, j: (b, 0)),
            pl.BlockSpec((None, PAST_LEN, nkv), lambda b, j: (b, 0, 0)),
            pl.BlockSpec((None, PAST_LEN, nkv), lambda b, j: (b, 0, 0)),
            pl.BlockSpec((D, D), lambda b, j: (0, 0)),
        ],
        out_specs=pl.BlockSpec((SWA_TQ, D), lambda b, j: (x_tile0 + b * nb + j, 0)),
        scratch_shapes=[pltpu.VMEM((SWA_TQ, D), BF16)],
        input_output_aliases={1: 0},
        compiler_params=_cparams(("arbitrary", "arbitrary")),
        name="swa_latent_attention",
    )(sink, x, mod_l, q, k, v, k_cache, v_cache, w_out)


DIFF_TQ = 256


def _diff_body(x_ref, mod_ref, q_ref, k_ref, v_ref, *rest, lam_init, cached):
    if cached:
        kc_ref, vc_ref = rest[:2]
        rest = rest[2:]
    lam_ref, sg_ref, wout_ref, o_ref, att_ref = rest
    tq = q_ref.shape[0]
    lane = lax.broadcasted_iota(I32, (tq, LANES), 1)
    low = lane < HEAD_DIM
    lv = lam_ref[...]
    lam = (jnp.exp(jnp.sum(lv[0:1] * lv[1:2], axis=-1, keepdims=True))
           - jnp.exp(jnp.sum(lv[2:3] * lv[3:4], axis=-1, keepdims=True)) + lam_init)
    sub_gain = sg_ref[...]
    for hd in range(DIFF_HEADS):
        lanes = slice(hd * LANES, (hd + 1) * LANES)
        qh = q_ref[:, lanes]
        vh = v_ref[:, lanes]
        zero = jnp.zeros_like(qh)
        q2 = jnp.concatenate([jnp.where(low, qh, zero), jnp.where(low, zero, qh)], axis=0)
        s = _dot_nt(q2, k_ref[:, lanes])
        mx = jnp.max(s, axis=-1, keepdims=True)
        if cached:
            s_c = _dot_nt(q2, kc_ref[:, lanes])
            mx = jnp.maximum(mx, jnp.max(s_c, axis=-1, keepdims=True))
        acc = _dot(jnp.exp(s - mx).astype(BF16), jnp.concatenate([vh, jnp.ones_like(vh)], axis=1))
        if cached:
            vch = vc_ref[:, lanes]
            acc += _dot(jnp.exp(s_c - mx).astype(BF16), jnp.concatenate([vch, jnp.ones_like(vch)], axis=1))
        r = acc[:, :LANES] / acc[:, LANES:LANES + 1]
        o = r[:tq] - lam * r[tq:]
        ms = jnp.mean(o * o, axis=-1, keepdims=True)
        o = (o * lax.rsqrt(ms + EPS) * sub_gain) * (1.0 - lam_init)
        att_ref[:, lanes] = o.astype(BF16)
    gate = mod_ref[2:3, :]
    o_ref[...] = x_ref[...] + gate * _dot(att_ref[...], wout_ref[...])


def _diff_attention(x, mod_l, q, k, v, cache, lam_vecs, sub_gain, w_out, *, latent, lam_init):
    tq = DIFF_TQ
    n_keys = k.shape[1]
    kv_map = lambda b, j: (b, 0, 0)
    kv_specs = [pl.BlockSpec((None, n_keys, D), kv_map), pl.BlockSpec((None, n_keys, D), kv_map)]
    kv_args = [k, v]
    if cache is not None:
        kv_specs += [pl.BlockSpec((None, PAST_LEN, D), kv_map), pl.BlockSpec((None, PAST_LEN, D), kv_map)]
        kv_args += list(cache)
    if latent:
        per_batch = DEC_SEQ // tq
        grid = (DEC_BATCH, per_batch)
        x_tile0 = T_CTX // tq
        x_map = lambda b, j: (x_tile0 + b * per_batch + j, 0)
        q_map = lambda b, j: (b * per_batch + j, 0)
        mod_map = lambda b, j: (1 + b, 0, 0)
    else:
        per_batch = SEQ // tq
        grid = (BATCH, per_batch)
        x_map = lambda b, j: (b * per_batch + j, 0)
        q_map = x_map
        mod_map = lambda b, j: (0, 0, 0)
    const2 = lambda b, j: (0, 0)
    return pl.pallas_call(
        functools.partial(_diff_body, lam_init=lam_init, cached=cache is not None),
        out_shape=jax.ShapeDtypeStruct((T, D), F32),
        grid=grid,
        in_specs=[
            pl.BlockSpec((tq, D), x_map),
            pl.BlockSpec((None, 6, D), mod_map),
            pl.BlockSpec((tq, D), q_map),
            *kv_specs,
            pl.BlockSpec((4, HEAD_DIM), const2),
            pl.BlockSpec((1, 2 * HEAD_DIM), const2),
            pl.BlockSpec((D, D), const2),
        ],
        out_specs=pl.BlockSpec((tq, D), x_map),
        scratch_shapes=[pltpu.VMEM((tq, D), BF16)],
        input_output_aliases={0: 0},
        compiler_params=_cparams(("arbitrary", "arbitrary")),
        name="diff_latent_attention" if latent else "diff_context_attention",
    )(x, mod_l, q, *kv_args, lam_vecs, sub_gain, w_out)


def _dispatch_body(x_ref, mod_ref, g_ref, wrt_hi_ref, wrt_lo_ref, brt_ref, before_ref, below_ref,
                   z_ref, pg_ref, meta_ref):
    tm = TM_MOE
    m = mod_ref[...]
    h = _norm_mod(x_ref[...], g_ref[...], m[4:5], m[3:4])
    hb = h.astype(BF16)
    h_lo = (h - hb.astype(F32)).astype(BF16)
    wrt_hi = wrt_hi_ref[...]
    logits_t = _dot_nt(wrt_hi, hb) + _dot_nt(wrt_hi, h_lo) + _dot_nt(wrt_lo_ref[...], hb)
    work = logits_t[:N_EXPERTS] + brt_ref[...]
    sub = lax.broadcasted_iota(I32, (N_EXPERTS, tm), 0)
    picks, vals = [], []
    for _ in range(TOP_K):
        best = jnp.max(work, axis=0, keepdims=True)
        idx = jnp.min(jnp.where(work == best, sub, N_EXPERTS), axis=0, keepdims=True)
        hit = sub == idx
        work = jnp.where(hit, -jnp.inf, work)
        picks.append(hit)
        vals.append(best)
    exps = [jnp.exp(v - vals[0]) for v in vals]
    den = exps[0] + exps[1] + exps[2] + exps[3]
    gates = [e / den for e in exps]
    chosen_b = jnp.where(picks[0] | picks[1] | picks[2] | picks[3], 1.0, 0.0).astype(BF16)

    rank_t = _dot(chosen_b, before_ref[...])
    count_c = _dot(chosen_b, jnp.ones((tm, LANES), BF16))
    chunks_c = jnp.floor((count_c + (CHUNK - 1)) * (1.0 / CHUNK))
    chunks_pad = jnp.concatenate([chunks_c, jnp.zeros((LANES - N_EXPERTS, LANES), F32)], axis=0)
    first_c = _dot(below_ref[...], chunks_pad.astype(BF16))[:N_EXPERTS]
    pos_full_t = jnp.concatenate([first_c] * (tm // LANES), axis=1) * CHUNK + rank_t
    pos_rows = [jnp.sum(jnp.where(p, pos_full_t, 0.0), axis=0, keepdims=True) for p in picks]

    row_id = lax.broadcasted_iota(I32, (LOCAL_ROWS, tm), 0).astype(F32)
    sel_t = jnp.zeros((LOCAL_ROWS, tm), F32)
    for k in range(TOP_K):
        sel_t = jnp.where(row_id == pos_rows[k], 1.0, sel_t)
    rows = _dot(sel_t.astype(BF16), hb)
    z_ref[...] = _chunks_to_storage(rows)

    sub_l = lax.broadcasted_iota(I32, (LANES, tm), 0)
    pg_t = jnp.zeros((LANES, tm), F32)
    for k in range(TOP_K):
        pg_t = jnp.where(sub_l == k, pos_rows[k], pg_t)
        pg_t = jnp.where(sub_l == TOP_K + k, gates[k], pg_t)
    pg_ref[...] = pg_t.T

    diag = (lax.broadcasted_iota(I32, (N_EXPERTS, LANES), 0) == lax.broadcasted_iota(I32, (N_EXPERTS, LANES), 1))
    cnt_row = jnp.sum(jnp.where(diag, chunks_c, 0.0), axis=0, keepdims=True)
    first_row = jnp.sum(jnp.where(diag, first_c, 0.0), axis=0, keepdims=True)
    sub8 = lax.broadcasted_iota(I32, (SUBLANES, LANES), 0)
    meta = jnp.where(sub8 == 0, cnt_row, jnp.where(sub8 == 1, first_row, 0.0))
    meta_ref[0] = meta.astype(I32)


def _moe_dispatch(x, mod_l, norm_g, w_router, b_router):
    wrt = jnp.pad(w_router.T, ((0, LANES - N_EXPERTS), (0, 0)))
    wrt_hi = wrt.astype(BF16)
    wrt_lo = (wrt - wrt_hi.astype(F32)).astype(BF16)
    brt = b_router.reshape(N_EXPERTS, 1)
    r = jnp.arange(TM_MOE)
    before = (r[:, None] < r[None, :]).astype(BF16)
    c = jnp.arange(LANES)
    below = (c[None, :] < c[:, None]).astype(BF16)
    const2 = lambda t: (0, 0)
    return pl.pallas_call(
        _dispatch_body,
        out_shape=[
            jax.ShapeDtypeStruct((N_TILE_MOE * LOCAL_CHUNKS, 2 * CHUNK, HALF), BF16),
            jax.ShapeDtypeStruct((T, LANES), F32),
            jax.ShapeDtypeStruct((N_TILE_MOE, SUBLANES, LANES), I32),
        ],
        grid=(N_TILE_MOE,),
        in_specs=[
            pl.BlockSpec((TM_MOE, D), lambda t: (t, 0)),
            pl.BlockSpec((None, 6, D), lambda t: (_cond_row(t, TM_MOE), 0, 0)),
            pl.BlockSpec((1, D), const2),
            pl.BlockSpec((LANES, D), const2),
            pl.BlockSpec((LANES, D), const2),
            pl.BlockSpec((N_EXPERTS, 1), const2),
            pl.BlockSpec((TM_MOE, TM_MOE), const2),
            pl.BlockSpec((LANES, LANES), const2),
        ],
        out_specs=[
            pl.BlockSpec((LOCAL_CHUNKS, 2 * CHUNK, HALF), lambda t: (t, 0, 0)),
            pl.BlockSpec((TM_MOE, LANES), lambda t: (t, 0)),
            pl.BlockSpec((1, SUBLANES, LANES), lambda t: (t, 0, 0)),
        ],
        compiler_params=_cparams(("arbitrary",)),
        name="moe_dispatch",
    )(x, mod_l, norm_g.reshape(1, D), wrt_hi, wrt_lo, brt, before, below)


def _expert_body(cnt_ref, first_ref, tstart_ref, ntile_ref, z_hbm, wgu_ref, bgu_ref, wd_ref, bd_ref, perm_ref,
                 y_hbm, wgu_s, wd_s, xbuf, ybuf, src_tab, sem_in, sem_out):
    e = pl.program_id(0)
    t0 = tstart_ref[e]
    nt = ntile_ref[e]
    total = tstart_ref[N_EXPERTS - 1] + ntile_ref[N_EXPERTS - 1]

    def build_source_table():
        def clear(p, carry):
            src_tab[p] = 0
            return carry

        def per_expert(ex, carry):
            def per_token_tile(tt, pos):
                n = cnt_ref[tt * N_EXPERTS + ex]
                first = tt * LOCAL_CHUNKS + first_ref[tt * N_EXPERTS + ex]

                def per_chunk(c, inner):
                    src_tab[pos + c] = first + c
                    return inner

                lax.fori_loop(0, n, per_chunk, 0)
                return pos + n

            pos = lax.fori_loop(0, N_TILE_MOE, per_token_tile, tstart_ref[ex] * EXP_CHUNKS)
            lax.fori_loop(pos, (tstart_ref[ex] + ntile_ref[ex]) * EXP_CHUNKS, clear, 0)
            return carry

        lax.fori_loop(0, N_EXPERTS, per_expert, 0)

    def start_gather(g, slot):
        base = jnp.minimum(g, total - 1) * EXP_CHUNKS
        for c in range(EXP_CHUNKS):
            pltpu.make_async_copy(z_hbm.at[src_tab[base + c]], xbuf.at[slot, c], sem_in.at[slot]).start(priority=1)

    def wait_gather(slot):
        pltpu.make_async_copy(z_hbm.at[pl.ds(0, EXP_CHUNKS)], xbuf.at[slot], sem_in.at[slot]).wait()

    def out_copy(g, slot):
        return pltpu.make_async_copy(ybuf.at[slot], y_hbm.at[pl.ds(g * EXP_CHUNKS, EXP_CHUNKS)],
                                     sem_out.at[slot])

    @pl.when(e == 0)
    def _():
        build_source_table()
        for g in range(N_XBUF - 1):
            start_gather(g, g)

    @pl.when(nt > 0)
    def _():
        for blk in range(2 * D_EXPERT // MXU_DIM):
            cols = slice(blk * MXU_DIM, (blk + 1) * MXU_DIM)
            wgu_s[:, cols] = _dot(wgu_ref[:, cols].astype(BF16), perm_ref[...]).astype(BF16)
        wd_s[...] = wd_ref[...].astype(BF16)

    def tile_body(i, carry):
        g = t0 + i
        slot = lax.rem(g, N_XBUF)
        oslot = lax.rem(g, 2)
        wait_gather(slot)
        x = _chunks_from_storage(xbuf[slot])
        acts = []
        for blk in range(2 * D_EXPERT // MXU_DIM):
            cols = slice(blk * MXU_DIM, (blk + 1) * MXU_DIM)
            gu = _dot(x, wgu_s[:, cols]) + bgu_ref[:, cols]
            x_glu = jnp.minimum(gu[:, :LANES], SWIGLU_LIMIT)
            x_lin = jnp.clip(gu[:, LANES:], -SWIGLU_LIMIT, SWIGLU_LIMIT)
            acts.append((x_glu * jax.nn.sigmoid(SWIGLU_ALPHA * x_glu) * (x_lin + 1.0)).astype(BF16))
        y = _dot(jnp.concatenate(acts, axis=1), wd_s[...]) + bd_ref[...]
        start_gather(g + N_XBUF - 1, lax.rem(g + N_XBUF - 1, N_XBUF))

        @pl.when(g >= 2)
        def _():
            out_copy(g - 2, oslot).wait()

        ybuf[oslot] = _chunks_to_storage(y)
        out_copy(g, oslot).start()
        return carry

    lax.fori_loop(0, nt, tile_body, 0)

    @pl.when(e == N_EXPERTS - 1)
    def _():
        for ahead in range(N_XBUF - 1):
            wait_gather(lax.rem(total + ahead, N_XBUF))
        last = total - 1
        out_copy(last, lax.rem(last, 2)).wait()

        @pl.when(total >= 2)
        def _():
            out_copy(last - 1, lax.rem(last - 1, 2)).wait()

        ybuf[0] = jnp.zeros((EXP_CHUNKS, 2 * CHUNK, HALF), BF16)

        def start_fill(g, carry):
            out_copy(g, 0).start()
            return carry

        def wait_fill(g, carry):
            out_copy(g, 0).wait()
            return carry

        lax.fori_loop(total, MAX_EXP_TILES, start_fill, 0)
        lax.fori_loop(total, MAX_EXP_TILES, wait_fill, 0)


def _moe_experts(z, cnt, first, tstart, ntile, layer, w_gate_up, bgu_all, w_down, b_down):
    n_gu = 2 * D_EXPERT
    i = jnp.arange(MXU_DIM)
    perm = (i[:, None] == jnp.where(i < LANES, 2 * i, 2 * (i - LANES) + 1)[None, :]).astype(BF16)
    wmap = lambda e, *_: (layer, e, 0, 0)
    grid_spec = pltpu.PrefetchScalarGridSpec(
        num_scalar_prefetch=4,
        grid=(N_EXPERTS,),
        in_specs=[
            pl.BlockSpec(memory_space=pl.ANY),
            pl.BlockSpec((None, None, D, n_gu), wmap),
            pl.BlockSpec((None, None, 1, n_gu), wmap),
            pl.BlockSpec((None, None, D_EXPERT, D), wmap),
            pl.BlockSpec((None, None, 1, D), wmap),
            pl.BlockSpec((MXU_DIM, MXU_DIM), lambda e, *_: (0, 0)),
        ],
        out_specs=pl.BlockSpec(memory_space=pl.ANY),
        scratch_shapes=[
            pltpu.VMEM((D, n_gu), BF16),
            pltpu.VMEM((D_EXPERT, D), BF16),
            pltpu.VMEM((N_XBUF, EXP_CHUNKS, 2 * CHUNK, HALF), BF16),
            pltpu.VMEM((2, EXP_CHUNKS, 2 * CHUNK, HALF), BF16),
            pltpu.SMEM((MAX_EXP_CHUNKS,), I32),
            pltpu.SemaphoreType.DMA((N_XBUF,)),
            pltpu.SemaphoreType.DMA((2,)),
        ],
    )
    return pl.pallas_call(
        _expert_body,
        out_shape=jax.ShapeDtypeStruct((MAX_EXP_CHUNKS, 2 * CHUNK, HALF), BF16),
        grid_spec=grid_spec,
        compiler_params=_cparams(("arbitrary",)),
        name="moe_experts",
    )(cnt, first, tstart, ntile, z, w_gate_up, bgu_all, w_down,
      b_down.reshape(DEPTH, N_EXPERTS, 1, D), perm)


def _combine_body(src_ref, x_ref, mod_ref, pg_ref, y_hbm, *rest, split):
    o_refs, (ybuf, sem) = rest[:-2], rest[-2:]
    t = pl.program_id(0)
    n_t = pl.num_programs(0)
    slot = lax.rem(t, N_YBUF)

    def start_gather(tile, sl):
        base = jnp.minimum(tile, n_t - 1) * LOCAL_CHUNKS
        for c in range(LOCAL_CHUNKS):
            pltpu.make_async_copy(y_hbm.at[src_ref[base + c]], ybuf.at[sl, c], sem.at[sl]).start(priority=c % 2)

    def wait_gather(sl):
        pltpu.make_async_copy(y_hbm.at[pl.ds(0, LOCAL_CHUNKS)], ybuf.at[sl], sem.at[sl]).wait()

    @pl.when(t == 0)
    def _():
        for ahead in range(N_YBUF - 1):
            start_gather(ahead, ahead)

    wait_gather(slot)
    y = _chunks_from_storage(ybuf[slot])
    pg = pg_ref[...]
    col = lax.broadcasted_iota(I32, (TM_MOE, LOCAL_ROWS), 1).astype(F32)
    comb = jnp.zeros((TM_MOE, LOCAL_ROWS), F32)
    for k in range(TOP_K):
        comb = jnp.where(col == pg[:, k:k + 1], pg[:, TOP_K + k:TOP_K + k + 1], comb)
    moe = _dot(comb.astype(BF16), y)
    ahead_tile = t + N_YBUF - 1
    start_gather(ahead_tile, lax.rem(ahead_tile, N_YBUF))
    new_x = x_ref[...] + mod_ref[5:6, :] * moe
    if split:
        ctx_tiles = T_CTX // TM_MOE

        @pl.when(t < ctx_tiles)
        def _():
            o_refs[0][...] = new_x

        @pl.when(t >= ctx_tiles)
        def _():
            o_refs[1][...] = new_x
    else:
        o_refs[0][...] = new_x

    @pl.when(t == n_t - 1)
    def _():
        for ahead in range(1, N_YBUF):
            wait_gather(lax.rem(t + ahead, N_YBUF))


def _moe_combine(x, mod_l, pg, y, src, *, split):
    ctx_tiles = T_CTX // TM_MOE
    if split:
        out_shape = [jax.ShapeDtypeStruct((T_CTX, D), F32), jax.ShapeDtypeStruct((T_LAT, D), F32)]
        out_specs = [
            pl.BlockSpec((TM_MOE, D), lambda t, *_: (jnp.minimum(t, ctx_tiles - 1), 0)),
            pl.BlockSpec((TM_MOE, D), lambda t, *_: (jnp.maximum(t - ctx_tiles, 0), 0)),
        ]
        aliases = {}
    else:
        out_shape = jax.ShapeDtypeStruct((T, D), F32)
        out_specs = pl.BlockSpec((TM_MOE, D), lambda t, *_: (t, 0))
        aliases = {1: 0}
    grid_spec = pltpu.PrefetchScalarGridSpec(
        num_scalar_prefetch=1,
        grid=(N_TILE_MOE,),
        in_specs=[
            pl.BlockSpec((TM_MOE, D), lambda t, *_: (t, 0)),
            pl.BlockSpec((None, 6, D), lambda t, *_: (_cond_row(t, TM_MOE), 0, 0)),
            pl.BlockSpec((TM_MOE, LANES), lambda t, *_: (t, 0)),
            pl.BlockSpec(memory_space=pl.ANY),
        ],
        out_specs=out_specs,
        scratch_shapes=[
            pltpu.VMEM((N_YBUF, LOCAL_CHUNKS, 2 * CHUNK, HALF), BF16),
            pltpu.SemaphoreType.DMA((N_YBUF,)),
        ],
    )
    return pl.pallas_call(
        functools.partial(_combine_body, split=split),
        out_shape=out_shape,
        grid_spec=grid_spec,
        input_output_aliases=aliases,
        compiler_params=_cparams(("arbitrary",)),
        name="moe_combine_final" if split else "moe_combine",
    )(src, x, mod_l, pg, y)


def _moe_layer(x, mod_l, norm_g, layer, w_router, b_router, w_gate_up, bgu_all, w_down, b_down, *, split):
    z, pg, meta = _moe_dispatch(x, mod_l, norm_g, w_router, b_router)
    cnt = meta[:, 0, :N_EXPERTS]
    first = meta[:, 1, :N_EXPERTS]
    ntile = (jnp.sum(cnt, axis=0) + EXP_CHUNKS - 1) // EXP_CHUNKS
    tstart = jnp.cumsum(ntile) - ntile
    y = _moe_experts(z, cnt.reshape(-1), first.reshape(-1), tstart, ntile, layer,
                     w_gate_up, bgu_all, w_down, b_down)
    end = first + cnt
    seen = jnp.cumsum(cnt, axis=0) - cnt
    j = jnp.arange(LOCAL_CHUNKS, dtype=I32)
    e_of_j = jnp.sum((end[:, None, :] <= j[None, :, None]).astype(I32), axis=-1)
    used = j[None, :] < end[:, -1:]
    own = jnp.arange(N_EXPERTS, dtype=I32)[None, None, :] == e_of_j[:, :, None]
    offset = tstart[None, :] * EXP_CHUNKS + seen - first
    src = j[None, :] + jnp.sum(jnp.where(own, offset[:, None, :], 0), axis=-1)
    src = jnp.where(used, src, 0).astype(I32).reshape(-1)
    return _moe_combine(x, mod_l, pg, y, src, split=split)


def kernel(x_prompt, x_sample, c, cache_swa_k, cache_swa_v, cache_diff_k, cache_diff_v, c_ctx, norm_mix_g, norm_ffn_g, ada_w, ada_b, conv_w_in, conv_w, conv_w_out, swa_w_qkv, swa_q_gain, swa_k_gain, swa_sink, swa_w_out, diff_w_qkv, diff_q_gain, diff_k_gain, diff_lam_q1, diff_lam_k1, diff_lam_q2, diff_lam_k2, diff_sub_gain, diff_w_out, moe_w_router, moe_b_router, moe_w_gate_up, moe_b_gate_up, moe_w_down, moe_b_down):
    x = (x_prompt.reshape(T_CTX, D), x_sample.reshape(T_LAT, D))
    cond = jnp.concatenate([c_ctx[None, :], c, jnp.zeros((COND_PAD - N_COND, D), F32)], axis=0)
    mod = _modulation(cond, ada_w, ada_b).reshape(DEPTH, COND_PAD, 6, D)
    rope_tabs = _rope_tables()
    n_gu = 2 * D_EXPERT
    bgu_all = moe_b_gate_up.reshape(DEPTH, N_EXPERTS, n_gu // MXU_DIM, LANES, 2)
    bgu_all = bgu_all.transpose(0, 1, 2, 4, 3).reshape(DEPTH, N_EXPERTS, 1, n_gu)
    new_kv = {}
    for i in range(DEPTH):
        j = i // 3
        kind = i % 3
        mod_l = mod[i]
        if kind == 0:
            x = _conv_layer(x, mod_l, norm_mix_g[i], conv_w_in[j].astype(BF16), conv_w[j],
                            conv_w_out[j].astype(BF16))
        elif kind == 1:
            w = swa_w_qkv[j].astype(BF16)
            w_out = swa_w_out[j].astype(BF16)
            nkv = SWA_KV_HEADS * HEAD_DIM
            dims = dict(nq=SWA_Q_HEADS * HEAD_DIM // LANES, nk=nkv // LANES, nv=nkv)
            q, k, v, kf, vf = _qkv_proj(x, mod_l, norm_mix_g[i], w, swa_q_gain[j], swa_k_gain[j], None,
                                        latent=False, **dims)
            new_kv["swa_k"] = kf.reshape(BATCH, 1, SEQ, SWA_KV_HEADS, HEAD_DIM)
            new_kv["swa_v"] = vf.reshape(BATCH, 1, SEQ, SWA_KV_HEADS, HEAD_DIM)
            x = _swa_ctx_attention(x, mod_l, q, k, v, swa_sink[j], w_out)
            q, k, v = _qkv_proj(x, mod_l, norm_mix_g[i], w, swa_q_gain[j], swa_k_gain[j], rope_tabs,
                                latent=True, **dims)
            kc = cache_swa_k[:, j].reshape(DEC_BATCH, PAST_LEN, nkv).astype(BF16)
            vc = cache_swa_v[:, j].reshape(DEC_BATCH, PAST_LEN, nkv).astype(BF16)
            x = _swa_lat_attention(x, mod_l, q, k, v, kc, vc, swa_sink[j], w_out)
        else:
            w = diff_w_qkv[j].astype(BF16)
            w_out = diff_w_out[j].astype(BF16)
            lam_init = 0.8 - 0.6 * math.exp(-0.3 * i)
            lam_vecs = jnp.stack([diff_lam_q1[j], diff_lam_k1[j], diff_lam_q2[j], diff_lam_k2[j]])
            sub_gain = diff_sub_gain[j].reshape(1, 2 * HEAD_DIM)
            dims = dict(nq=D // LANES, nk=D // LANES, nv=D)
            q, k, v, kf, vf = _qkv_proj(x, mod_l, norm_mix_g[i], w, diff_q_gain[j], diff_k_gain[j], None,
                                        latent=False, **dims)
            new_kv["diff_k"] = kf.reshape(BATCH, 1, SEQ, DIFF_HEADS, 2, HEAD_DIM)
            new_kv["diff_v"] = vf.reshape(BATCH, 1, SEQ, DIFF_HEADS, 2 * HEAD_DIM)
            x = _diff_attention(x, mod_l, q, k.reshape(BATCH, SEQ, D), v.reshape(BATCH, SEQ, D), None,
                                lam_vecs, sub_gain, w_out, latent=False, lam_init=lam_init)
            q, k, v = _qkv_proj(x, mod_l, norm_mix_g[i], w, diff_q_gain[j], diff_k_gain[j], rope_tabs,
                                latent=True, **dims)
            kc = cache_diff_k[:, j].reshape(DEC_BATCH, PAST_LEN, D).astype(BF16)
            vc = cache_diff_v[:, j].reshape(DEC_BATCH, PAST_LEN, D).astype(BF16)
            x = _diff_attention(x, mod_l, q, k.reshape(DEC_BATCH, DEC_SEQ, D), v.reshape(DEC_BATCH, DEC_SEQ, D),
                                (kc, vc), lam_vecs, sub_gain, w_out, latent=True, lam_init=lam_init)
        x = _moe_layer(x, mod_l, norm_ffn_g[i], i, moe_w_router[i], moe_b_router[i], moe_w_gate_up,
                       bgu_all, moe_w_down, moe_b_down, split=i == DEPTH - 1)
    y_prompt, y_sample = x
    return (y_prompt.reshape(BATCH, SEQ, D), y_sample.reshape(DEC_BATCH, DEC_SEQ, D),
            new_kv["swa_k"], new_kv["swa_v"], new_kv["diff_k"], new_kv["diff_v"])
```

```python
import functools
import math

import jax
import jax.numpy as jnp
from jax import lax
from jax.experimental import pallas as pl
from jax.experimental.pallas import tpu as pltpu

F32 = jnp.float32
BF16 = jnp.bfloat16
I32 = jnp.int32

D = 1024
BATCH = 32
SEQ = 256
DEC_BATCH = 4
DEC_SEQ = 2048
PAST_LEN = 512
DEPTH = 4
GRID_W = 64
HEAD_DIM = 64
SWA_Q_HEADS = 16
SWA_KV_HEADS = 4
SWA_GROUP = SWA_Q_HEADS // SWA_KV_HEADS
SWA_WINDOW = 128
DIFF_HEADS = 8
N_EXPERTS = 32
TOP_K = 4
D_EXPERT = 1024
SWIGLU_ALPHA = 1.702
SWIGLU_LIMIT = 7.0
ROPE_BASE = 10000.0
EPS = 1e-6

T_CTX = BATCH * SEQ
T_LAT = DEC_BATCH * DEC_SEQ
T = T_CTX + T_LAT
N_COND = 1 + DEC_BATCH
COND_PAD = 8

LANES = 128
SUBLANES = 8
BF16_ROWS = 16
MXU_DIM = 256

TM_CONV = 512
TM_QKV = 512
TM_MOE = 256
TM_EXP = 256
CHUNK = SUBLANES
HALF = D // 2
assert 2 * CHUNK == BF16_ROWS
N_TILE_MOE = T // TM_MOE
LOCAL_ROWS = TM_MOE * TOP_K + N_EXPERTS * (CHUNK - 1)
LOCAL_ROWS = -(-LOCAL_ROWS // LANES) * LANES
LOCAL_CHUNKS = LOCAL_ROWS // CHUNK
EXP_CHUNKS = TM_EXP // CHUNK
MAX_GROUP_ROWS = T * TOP_K + N_TILE_MOE * N_EXPERTS * (CHUNK - 1)
MAX_EXP_TILES = -(-MAX_GROUP_ROWS // TM_EXP) + N_EXPERTS
MAX_EXP_CHUNKS = MAX_EXP_TILES * EXP_CHUNKS
N_XBUF = 3
N_YBUF = 3
TABLE_RUN = 8


def _chunks_to_storage(rows):
    r3 = rows.reshape(rows.shape[0] // CHUNK, CHUNK, D)
    return jnp.concatenate([r3[:, :, :HALF], r3[:, :, HALF:]], axis=1).astype(BF16)


def _chunks_from_storage(stored):
    s = stored.astype(F32)
    r3 = jnp.concatenate([s[:, :CHUNK, :], s[:, CHUNK:, :]], axis=-1)
    return r3.reshape(stored.shape[0] * CHUNK, D).astype(BF16)

VMEM_LIMIT = 56 * 1024 * 1024


def _cparams(sem, vmem=VMEM_LIMIT):
    return pltpu.CompilerParams(dimension_semantics=sem, vmem_limit_bytes=vmem)


def _dot(a, b):
    return jnp.dot(a, b, preferred_element_type=F32)


def _dot_nt(a, b):
    return lax.dot_general(a, b, (((1,), (1,)), ((), ())), preferred_element_type=F32)


def _dot_tn(a, b):
    return lax.dot_general(a, b, (((0,), (0,)), ((), ())), preferred_element_type=F32)


def _norm_mod(x, g, scale, shift):
    ms = jnp.mean(x * x, axis=-1, keepdims=True)
    y = x * lax.rsqrt(ms + EPS)
    return (y * g) * (1.0 + scale) + shift


MOD_TN = 1536


def _mod_body(cond_ref, w_ref, b_ref, o_ref):
    c = cond_ref[...]
    s = (c * jax.nn.sigmoid(c)).astype(BF16)
    o_ref[0] = _dot(s, w_ref[0].astype(BF16)) + b_ref[0]


def _modulation(cond, ada_w, ada_b):
    n_layer = ada_w.shape[0]
    n_out = ada_w.shape[2]
    return pl.pallas_call(
        _mod_body,
        out_shape=jax.ShapeDtypeStruct((n_layer, COND_PAD, n_out), F32),
        grid=(n_layer, n_out // MOD_TN),
        in_specs=[
            pl.BlockSpec((COND_PAD, D), lambda i, j: (0, 0)),
            pl.BlockSpec((1, D, MOD_TN), lambda i, j: (i, 0, j)),
            pl.BlockSpec((1, 1, MOD_TN), lambda i, j: (i, 0, j)),
        ],
        out_specs=pl.BlockSpec((1, COND_PAD, MOD_TN), lambda i, j: (i, 0, j)),
        compiler_params=_cparams(("arbitrary", "arbitrary")),
        name="modulation",
    )(cond, ada_w, ada_b.reshape(n_layer, 1, n_out))


def _cond_row(row_tile, tile_rows):
    ctx_tiles = T_CTX // tile_rows
    per_batch = DEC_SEQ // tile_rows
    return jnp.where(row_tile < ctx_tiles, 0, 1 + (row_tile - ctx_tiles) // per_batch)


def _conv_body(*refs, split):
    t = pl.program_id(0)
    if split:
        xc_ref, xl_ref, xcp_ref, xlp_ref, xcn_ref, xln_ref = refs[:6]
        refs = refs[6:]
        is_ctx = t < T_CTX // xc_ref.shape[0]
        x = jnp.where(is_ctx, xc_ref[...], xl_ref[...])
        x_before = jnp.where(is_ctx, xcp_ref[...], xlp_ref[...])
        x_after = jnp.where(is_ctx, xcn_ref[...], xln_ref[...])
    else:
        x_ref, xp_ref, xn_ref = refs[:3]
        refs = refs[3:]
        x, x_before, x_after = x_ref[...], xp_ref[...], xn_ref[...]
    mod_ref, g_ref, win_ref, cw_ref, wout_ref, o_ref = refs
    tm = x.shape[0]
    m = mod_ref[...]
    shift, scale, gate = m[0:1], m[1:2], m[2:3]
    x_ext = jnp.concatenate([x_before, x, x_after], axis=0)
    h = _norm_mod(x_ext, g_ref[...], scale, shift).astype(BF16)
    proj = _dot(h, win_ref[...])
    gate_b = proj[SUBLANES:SUBLANES + tm, :D]
    u_ext = proj[:, D:2 * D] * proj[:, 2 * D:]
    n_ext = tm + 2 * SUBLANES
    u = u_ext[SUBLANES:SUBLANES + tm]
    u_prev = pltpu.roll(u_ext, 1, axis=0)[SUBLANES:SUBLANES + tm]
    u_next = pltpu.roll(u_ext, n_ext - 1, axis=0)[SUBLANES:SUBLANES + tm]
    row = lax.broadcasted_iota(I32, (tm, 1), 0)
    grow = t * tm + row
    seq_mask = jnp.where(grow < T_CTX, SEQ - 1, DEC_SEQ - 1)
    pos = grow & seq_mask
    u_prev = jnp.where(pos == 0, 0.0, u_prev)
    u_next = jnp.where(pos == seq_mask, 0.0, u_next)
    cw = cw_ref[...]
    conv = u_prev * cw[0:1] + u * cw[1:2] + u_next * cw[2:3]
    y = (gate_b * conv).astype(BF16)
    o_ref[...] = x + gate * _dot(y, wout_ref[...])


def _conv_layer(xs, mod_l, norm_g, w_in, conv_w, w_out):
    tm = TM_CONV
    n_tiles = T // tm
    sub_per_tile = tm // SUBLANES
    split = isinstance(xs, tuple)
    if split:
        x_specs, x_args = [], []
        for kind in ("tile", "before", "after"):
            for x_part, tile0 in zip(xs, (0, T_CTX // tm)):
                part_tiles = x_part.shape[0] // tm
                last_sub = x_part.shape[0] // SUBLANES - 1
                if kind == "tile":
                    spec = pl.BlockSpec((tm, D), functools.partial(
                        lambda t, tile0, part_tiles: (jnp.clip(t - tile0, 0, part_tiles - 1), 0),
                        tile0=tile0, part_tiles=part_tiles))
                else:
                    shift = -1 if kind == "before" else sub_per_tile
                    spec = pl.BlockSpec((SUBLANES, D), functools.partial(
                        lambda t, tile0, last_sub, shift: (jnp.clip((t - tile0) * sub_per_tile + shift, 0, last_sub), 0),
                        tile0=tile0, last_sub=last_sub, shift=shift))
                x_specs.append(spec)
                x_args.append(x_part)
    else:
        last_sub = T // SUBLANES - 1
        x_specs = [
            pl.BlockSpec((tm, D), lambda t: (t, 0)),
            pl.BlockSpec((SUBLANES, D), lambda t: (jnp.maximum(t * sub_per_tile - 1, 0), 0)),
            pl.BlockSpec((SUBLANES, D), lambda t: (jnp.minimum((t + 1) * sub_per_tile, last_sub), 0)),
        ]
        x_args = [xs, xs, xs]
    return pl.pallas_call(
        functools.partial(_conv_body, split=split),
        out_shape=jax.ShapeDtypeStruct((T, D), F32),
        grid=(n_tiles,),
        in_specs=x_specs + [
            pl.BlockSpec((None, 6, D), lambda t: (_cond_row(t, tm), 0, 0)),
            pl.BlockSpec((1, D), lambda t: (0, 0)),
            pl.BlockSpec((D, 3 * D), lambda t: (0, 0)),
            pl.BlockSpec((3, D), lambda t: (0, 0)),
            pl.BlockSpec((D, D), lambda t: (0, 0)),
        ],
        out_specs=pl.BlockSpec((tm, D), lambda t: (t, 0)),
        compiler_params=_cparams(("arbitrary",)),
        name="conv_mixer",
    )(*x_args, mod_l, norm_g.reshape(1, D), w_in, conv_w, w_out)


def _head_norm(blk, gain):
    lane = lax.broadcasted_iota(I32, blk.shape, 1)
    low = lane < HEAD_DIM
    s = blk * blk
    s_lo = jnp.sum(jnp.where(low, s, 0.0), axis=-1, keepdims=True)
    s_hi = jnp.sum(jnp.where(low, 0.0, s), axis=-1, keepdims=True)
    ms = jnp.where(low, s_lo, s_hi) * (1.0 / HEAD_DIM)
    return blk * lax.rsqrt(ms + EPS) * gain


def _rope(blk, cos, sin_signed):
    lane = lax.broadcasted_iota(I32, blk.shape, 1)
    first = (lane & 31) < 16
    partner = jnp.where(first, pltpu.roll(blk, LANES - 16, axis=1), pltpu.roll(blk, 16, axis=1))
    return blk * cos + partner * sin_signed


def _qkv_body(*refs, nq, nk, nv, rope, emit_f32, q_scale):
    refs = list(refs)
    x_ref, mod_ref, g_ref, w_ref, qg_ref, kg_ref = refs[:6]
    refs = refs[6:]
    if rope:
        cos_ref, sin_ref = refs[:2]
        refs = refs[2:]
    q_ref, k_ref, v_ref = refs[:3]
    refs = refs[3:]
    if emit_f32:
        kf_ref, vf_ref = refs[:2]
    x = x_ref[...]
    m = mod_ref[...]
    h = _norm_mod(x, g_ref[...], m[1:2], m[0:1]).astype(BF16)
    proj = _dot(h, w_ref[...])
    if rope:
        cos = cos_ref[...]
        sin = sin_ref[...]
    for j in range(nq):
        blk = _head_norm(proj[:, j * LANES:(j + 1) * LANES], qg_ref[...])
        if rope:
            blk = _rope(blk, cos, sin)
        q_ref[:, j * LANES:(j + 1) * LANES] = (blk * q_scale).astype(BF16)
    off = nq * LANES
    for j in range(nk):
        blk = _head_norm(proj[:, off + j * LANES:off + (j + 1) * LANES], kg_ref[...])
        if emit_f32:
            kf_ref[:, j * LANES:(j + 1) * LANES] = blk
        if rope:
            blk = _rope(blk, cos, sin)
        k_ref[:, j * LANES:(j + 1) * LANES] = blk.astype(BF16)
    off = (nq + nk) * LANES
    v = proj[:, off:off + nv]
    v_ref[...] = v.astype(BF16)
    if emit_f32:
        vf_ref[...] = v


def _qkv_proj(x, mod_l, norm_g, w, q_gain, k_gain, rope_tabs, *, latent, nq, nk, nv):
    tm = TM_QKV
    n_rows = T_LAT if latent else T_CTX
    n_tiles = n_rows // tm
    tile0 = (T_CTX // tm) if latent else 0
    n_proj = (nq + nk) * LANES + nv
    gain2 = lambda g: jnp.concatenate([g, g]).reshape(1, LANES)
    in_specs = [
        pl.BlockSpec((tm, D), lambda t: (t + tile0, 0)),
        pl.BlockSpec((None, 6, D), lambda t: (_cond_row(t + tile0, tm), 0, 0)),
        pl.BlockSpec((1, D), lambda t: (0, 0)),
        pl.BlockSpec((D, n_proj), lambda t: (0, 0)),
        pl.BlockSpec((1, LANES), lambda t: (0, 0)),
        pl.BlockSpec((1, LANES), lambda t: (0, 0)),
    ]
    args = [x, mod_l, norm_g.reshape(1, D), w, gain2(q_gain), gain2(k_gain)]
    if latent:
        per_seq = DEC_SEQ // tm
        in_specs += [pl.BlockSpec((tm, LANES), lambda t: (t % per_seq, 0))] * 2
        args += list(rope_tabs)
    out_shape = [
        jax.ShapeDtypeStruct((n_rows, nq * LANES), BF16),
        jax.ShapeDtypeStruct((n_rows, nk * LANES), BF16),
        jax.ShapeDtypeStruct((n_rows, nv), BF16),
    ]
    out_specs = [
        pl.BlockSpec((tm, nq * LANES), lambda t: (t, 0)),
        pl.BlockSpec((tm, nk * LANES), lambda t: (t, 0)),
        pl.BlockSpec((tm, nv), lambda t: (t, 0)),
    ]
    if not latent:
        out_shape += [jax.ShapeDtypeStruct((n_rows, nk * LANES), F32), jax.ShapeDtypeStruct((n_rows, nv), F32)]
        out_specs += [pl.BlockSpec((tm, nk * LANES), lambda t: (t, 0)), pl.BlockSpec((tm, nv), lambda t: (t, 0))]
    body = functools.partial(_qkv_body, nq=nq, nk=nk, nv=nv, rope=latent, emit_f32=not latent,
                             q_scale=HEAD_DIM ** -0.5)
    return pl.pallas_call(
        body,
        out_shape=out_shape,
        grid=(n_tiles,),
        in_specs=in_specs,
        out_specs=out_specs,
        compiler_params=_cparams(("arbitrary",)),
        name="qkv_latent" if latent else "qkv_context",
    )(*args)


def _rope_tables():
    rows = DEC_SEQ // GRID_W
    row_pos = jnp.repeat(jnp.arange(rows, dtype=F32), GRID_W)
    col_pos = jnp.tile(jnp.arange(GRID_W, dtype=F32), rows)
    half = HEAD_DIM // 2
    inv_freq = ROPE_BASE ** (-jnp.arange(0, half, 2, dtype=F32) / half)
    ang_r = row_pos[:, None] * inv_freq[None, :]
    ang_c = col_pos[:, None] * inv_freq[None, :]
    cr, sr, cc, sc = jnp.cos(ang_r), jnp.sin(ang_r), jnp.cos(ang_c), jnp.sin(ang_c)
    cos64 = jnp.concatenate([cr, cr, cc, cc], axis=-1)
    sin64 = jnp.concatenate([-sr, sr, -sc, sc], axis=-1)
    return jnp.concatenate([cos64, cos64], axis=-1), jnp.concatenate([sin64, sin64], axis=-1)


def _swap_halves(blk):
    return pltpu.roll(blk.astype(F32), HEAD_DIM, axis=1).astype(BF16)


def _kv_operands(kh, k_blk, v_blk):
    lane = lax.broadcasted_iota(I32, k_blk.shape, 1)
    low = lane < HEAD_DIM
    k_swp = _swap_halves(k_blk)
    if kh % 2 == 0:
        k2 = jnp.where(low, k_blk, k_swp)
        v_lo = v_blk
    else:
        k2 = jnp.where(low, k_swp, k_blk)
        v_lo = _swap_halves(v_blk)
    v2 = jnp.where(low, v_lo, jnp.ones_like(v_lo))
    return k2, v2


def _stack_query_heads(q_ref, kh, tq):
    lane = lax.broadcasted_iota(I32, (tq, LANES), 1)
    low = lane < HEAD_DIM
    parts = []
    for pair in range(SWA_GROUP // 2):
        blk = (SWA_GROUP // 2) * kh + pair
        qp = q_ref[:, blk * LANES:(blk + 1) * LANES]
        zero = jnp.zeros_like(qp)
        parts += [jnp.where(low, qp, zero), jnp.where(low, zero, qp)]
    return jnp.concatenate(parts, axis=0)


def _sink_column(sink_ref, kh, tq):
    row = lax.broadcasted_iota(I32, (SWA_GROUP * tq, 1), 0)
    col = jnp.full((SWA_GROUP * tq, 1), sink_ref[SWA_GROUP * kh], F32)
    for g in range(1, SWA_GROUP):
        col = jnp.where(row >= g * tq, sink_ref[SWA_GROUP * kh + g], col)
    return col


def _store_head_group(att_ref, kh, tq, acc, sink_term):
    lane = lax.broadcasted_iota(I32, (tq, LANES), 1)
    low = lane < HEAD_DIM
    den = acc[:, HEAD_DIM:HEAD_DIM + 1] + sink_term
    r = acc / den
    r_hi = pltpu.roll(r, HEAD_DIM, axis=1)
    for pair in range(SWA_GROUP // 2):
        blk = (SWA_GROUP // 2) * kh + pair
        a = r[(2 * pair) * tq:(2 * pair + 1) * tq]
        b = r_hi[(2 * pair + 1) * tq:(2 * pair + 2) * tq]
        att_ref[:, blk * LANES:(blk + 1) * LANES] = jnp.where(low, a, b).astype(BF16)


def _swa_ctx_body(sink_ref, x_ref, mod_ref, q_ref, k_ref, v_ref, wout_ref, o_ref, att_ref):
    tq = q_ref.shape[0]
    for kh in range(SWA_KV_HEADS):
        blk = kh // 2
        k2, v2 = _kv_operands(kh, k_ref[:, blk * LANES:(blk + 1) * LANES], v_ref[:, blk * LANES:(blk + 1) * LANES])
        qs = _stack_query_heads(q_ref, kh, tq)
        sink = _sink_column(sink_ref, kh, tq)
        s = _dot_nt(qs, k2)
        mx = jnp.maximum(jnp.max(s, axis=-1, keepdims=True), sink)
        p = jnp.exp(s - mx)
        acc = _dot(p.astype(BF16), v2)
        _store_head_group(att_ref, kh, tq, acc, jnp.exp(sink - mx))
    gate = mod_ref[2:3, :]
    o_ref[...] = x_ref[...] + gate * _dot(att_ref[...], wout_ref[...])


def _swa_ctx_attention(x, mod_l, q, k, v, sink, w_out):
    nkv = SWA_KV_HEADS * HEAD_DIM
    return pl.pallas_call(
        _swa_ctx_body,
        out_shape=jax.ShapeDtypeStruct((T, D), F32),
        grid=(BATCH,),
        in_specs=[
            pl.BlockSpec(memory_space=pltpu.SMEM),
            pl.BlockSpec((SEQ, D), lambda b: (b, 0)),
            pl.BlockSpec((None, 6, D), lambda b: (0, 0, 0)),
            pl.BlockSpec((SEQ, D), lambda b: (b, 0)),
            pl.BlockSpec((SEQ, nkv), lambda b: (b, 0)),
            pl.BlockSpec((SEQ, nkv), lambda b: (b, 0)),
            pl.BlockSpec((D, D), lambda b: (0, 0)),
        ],
        out_specs=pl.BlockSpec((SEQ, D), lambda b: (b, 0)),
        scratch_shapes=[pltpu.VMEM((SEQ, D), BF16)],
        input_output_aliases={1: 0},
        compiler_params=_cparams(("arbitrary",)),
        name="swa_context_attention",
    )(sink, x, mod_l, q, k, v, w_out)


SWA_TQ = 128
SWA_KWIN = 3 * SWA_TQ


def _swa_lat_body(sink_ref, x_ref, mod_ref, q_ref, k_ref, v_ref, kc_ref, vc_ref, wout_ref, o_ref, att_ref):
    j = pl.program_id(1)
    tq = SWA_TQ
    start = jnp.clip((j - 1) * tq, 0, DEC_SEQ - SWA_KWIN)
    start = pl.multiple_of(start, tq)
    row = lax.broadcasted_iota(I32, (SWA_GROUP * tq, 1), 0)
    q_pos = j * tq + (row & (tq - 1))
    k_pos = start + lax.broadcasted_iota(I32, (1, SWA_KWIN), 1)
    valid = jnp.abs(k_pos - q_pos) <= SWA_WINDOW
    for kh in range(SWA_KV_HEADS):
        blk = kh // 2
        lanes = slice(blk * LANES, (blk + 1) * LANES)
        k2, v2 = _kv_operands(kh, k_ref[pl.ds(start, SWA_KWIN), lanes], v_ref[pl.ds(start, SWA_KWIN), lanes])
        kc2, vc2 = _kv_operands(kh, kc_ref[:, lanes], vc_ref[:, lanes])
        qs = _stack_query_heads(q_ref, kh, tq)
        sink = _sink_column(sink_ref, kh, tq)
        s_lat = jnp.where(valid, _dot_nt(qs, k2), -jnp.inf)
        s_ctx = _dot_nt(qs, kc2)
        mx = jnp.maximum(jnp.max(s_lat, axis=-1, keepdims=True), jnp.max(s_ctx, axis=-1, keepdims=True))
        mx = jnp.maximum(mx, sink)
        p_lat = jnp.exp(s_lat - mx).astype(BF16)
        p_ctx = jnp.exp(s_ctx - mx).astype(BF16)
        acc = _dot(p_lat, v2) + _dot(p_ctx, vc2)
        _store_head_group(att_ref, kh, tq, acc, jnp.exp(sink - mx))
    gate = mod_ref[2:3, :]
    o_ref[...] = x_ref[...] + gate * _dot(att_ref[...], wout_ref[...])


def _swa_lat_attention(x, mod_l, q, k, v, k_cache, v_cache, sink, w_out):
    nkv = SWA_KV_HEADS * HEAD_DIM
    nb = DEC_SEQ // SWA_TQ
    x_tile0 = T_CTX // SWA_TQ
    return pl.pallas_call(
        _swa_lat_body,
        out_shape=jax.ShapeDtypeStruct((T, D), F32),
        grid=(DEC_BATCH, nb),
        in_specs=[
            pl.BlockSpec(memory_space=pltpu.SMEM),
            pl.BlockSpec((SWA_TQ, D), lambda b, j: (x_tile0 + b * nb + j, 0)),
            pl.BlockSpec((None, 6, D), lambda b, j: (1 + b, 0, 0)),
            pl.BlockSpec((SWA_TQ, D), lambda b, j: (b * nb + j, 0)),
            pl.BlockSpec((DEC_SEQ, nkv), lambda b, j: (b, 0)),
            pl.BlockSpec((DEC_SEQ, nkv), lambda b, j: (b, 0)),
            pl.BlockSpec((None, PAST_LEN, nkv), lambda b, j: (b, 0, 0)),
            pl.BlockSpec((None, PAST_LEN, nkv), lambda b, j: (b, 0, 0)),
            pl.BlockSpec((D, D), lambda b, j: (0, 0)),
        ],
        out_specs=pl.BlockSpec((SWA_TQ, D), lambda b, j: (x_tile0 + b * nb + j, 0)),
        scratch_shapes=[pltpu.VMEM((SWA_TQ, D), BF16)],
        input_output_aliases={1: 0},
        compiler_params=_cparams(("arbitrary", "arbitrary")),
        name="swa_latent_attention",
    )(sink, x, mod_l, q, k, v, k_cache, v_cache, w_out)


DIFF_TQ = 256


def _diff_body(x_ref, mod_ref, q_ref, k_ref, v_ref, *rest, lam_init, cached):
    if cached:
        kc_ref, vc_ref = rest[:2]
        rest = rest[2:]
    lam_ref, sg_ref, wout_ref, o_ref, att_ref = rest
    tq = q_ref.shape[0]
    lane = lax.broadcasted_iota(I32, (tq, LANES), 1)
    low = lane < HEAD_DIM
    lv = lam_ref[...]
    lam = (jnp.exp(jnp.sum(lv[0:1] * lv[1:2], axis=-1, keepdims=True))
           - jnp.exp(jnp.sum(lv[2:3] * lv[3:4], axis=-1, keepdims=True)) + lam_init)
    sub_gain = sg_ref[...]
    for hd in range(DIFF_HEADS):
        lanes = slice(hd * LANES, (hd + 1) * LANES)
        qh = q_ref[:, lanes]
        vh = v_ref[:, lanes]
        zero = jnp.zeros_like(qh)
        q2 = jnp.concatenate([jnp.where(low, qh, zero), jnp.where(low, zero, qh)], axis=0)
        s = _dot_nt(q2, k_ref[:, lanes])
        mx = jnp.max(s, axis=-1, keepdims=True)
        if cached:
            s_c = _dot_nt(q2, kc_ref[:, lanes])
            mx = jnp.maximum(mx, jnp.max(s_c, axis=-1, keepdims=True))
        acc = _dot(jnp.exp(s - mx).astype(BF16), jnp.concatenate([vh, jnp.ones_like(vh)], axis=1))
        if cached:
            vch = vc_ref[:, lanes]
            acc += _dot(jnp.exp(s_c - mx).astype(BF16), jnp.concatenate([vch, jnp.ones_like(vch)], axis=1))
        r = acc[:, :LANES] / acc[:, LANES:LANES + 1]
        o = r[:tq] - lam * r[tq:]
        ms = jnp.mean(o * o, axis=-1, keepdims=True)
        o = (o * lax.rsqrt(ms + EPS) * sub_gain) * (1.0 - lam_init)
        att_ref[:, lanes] = o.astype(BF16)
    gate = mod_ref[2:3, :]
    o_ref[...] = x_ref[...] + gate * _dot(att_ref[...], wout_ref[...])


def _diff_attention(x, mod_l, q, k, v, cache, lam_vecs, sub_gain, w_out, *, latent, lam_init):
    tq = DIFF_TQ
    n_keys = k.shape[1]
    kv_map = lambda b, j: (b, 0, 0)
    kv_specs = [pl.BlockSpec((None, n_keys, D), kv_map), pl.BlockSpec((None, n_keys, D), kv_map)]
    kv_args = [k, v]
    if cache is not None:
        kv_specs += [pl.BlockSpec((None, PAST_LEN, D), kv_map), pl.BlockSpec((None, PAST_LEN, D), kv_map)]
        kv_args += list(cache)
    if latent:
        per_batch = DEC_SEQ // tq
        grid = (DEC_BATCH, per_batch)
        x_tile0 = T_CTX // tq
        x_map = lambda b, j: (x_tile0 + b * per_batch + j, 0)
        q_map = lambda b, j: (b * per_batch + j, 0)
        mod_map = lambda b, j: (1 + b, 0, 0)
    else:
        per_batch = SEQ // tq
        grid = (BATCH, per_batch)
        x_map = lambda b, j: (b * per_batch + j, 0)
        q_map = x_map
        mod_map = lambda b, j: (0, 0, 0)
    const2 = lambda b, j: (0, 0)
    return pl.pallas_call(
        functools.partial(_diff_body, lam_init=lam_init, cached=cache is not None),
        out_shape=jax.ShapeDtypeStruct((T, D), F32),
        grid=grid,
        in_specs=[
            pl.BlockSpec((tq, D), x_map),
            pl.BlockSpec((None, 6, D), mod_map),
            pl.BlockSpec((tq, D), q_map),
            *kv_specs,
            pl.BlockSpec((4, HEAD_DIM), const2),
            pl.BlockSpec((1, 2 * HEAD_DIM), const2),
            pl.BlockSpec((D, D), const2),
        ],
        out_specs=pl.BlockSpec((tq, D), x_map),
        scratch_shapes=[pltpu.VMEM((tq, D), BF16)],
        input_output_aliases={0: 0},
        compiler_params=_cparams(("arbitrary", "arbitrary")),
        name="diff_latent_attention" if latent else "diff_context_attention",
    )(x, mod_l, q, *kv_args, lam_vecs, sub_gain, w_out)


def _dispatch_body(x_ref, mod_ref, g_ref, wrt_hi_ref, wrt_lo_ref, brt_ref, before_ref, below_ref,
                   z_ref, pg_ref, meta_ref):
    tm = TM_MOE
    m = mod_ref[...]
    h = _norm_mod(x_ref[...], g_ref[...], m[4:5], m[3:4])
    hb = h.astype(BF16)
    h_lo = (h - hb.astype(F32)).astype(BF16)
    wrt_hi = wrt_hi_ref[...]
    logits_t = _dot_nt(wrt_hi, hb) + _dot_nt(wrt_hi, h_lo) + _dot_nt(wrt_lo_ref[...], hb)
    work = logits_t[:N_EXPERTS] + brt_ref[...]
    sub = lax.broadcasted_iota(I32, (N_EXPERTS, tm), 0)
    picks, vals = [], []
    for _ in range(TOP_K):
        best = jnp.max(work, axis=0, keepdims=True)
        idx = jnp.min(jnp.where(work == best, sub, N_EXPERTS), axis=0, keepdims=True)
        hit = sub == idx
        work = jnp.where(hit, -jnp.inf, work)
        picks.append(hit)
        vals.append(best)
    exps = [jnp.exp(v - vals[0]) for v in vals]
    den = exps[0] + exps[1] + exps[2] + exps[3]
    gates = [e / den for e in exps]
    chosen_b = jnp.where(picks[0] | picks[1] | picks[2] | picks[3], 1.0, 0.0).astype(BF16)

    rank_t = _dot(chosen_b, before_ref[...])
    count_c = _dot(chosen_b, jnp.ones((tm, LANES), BF16))
    chunks_c = jnp.floor((count_c + (CHUNK - 1)) * (1.0 / CHUNK))
    chunks_pad = jnp.concatenate([chunks_c, jnp.zeros((LANES - N_EXPERTS, LANES), F32)], axis=0)
    first_c = _dot(below_ref[...], chunks_pad.astype(BF16))[:N_EXPERTS]
    pos_full_t = jnp.concatenate([first_c] * (tm // LANES), axis=1) * CHUNK + rank_t
    pos_rows = [jnp.sum(jnp.where(p, pos_full_t, 0.0), axis=0, keepdims=True) for p in picks]

    row_id = lax.broadcasted_iota(I32, (LOCAL_ROWS, tm), 0).astype(F32)
    sel_t = jnp.zeros((LOCAL_ROWS, tm), F32)
    for k in range(TOP_K):
        sel_t = jnp.where(row_id == pos_rows[k], 1.0, sel_t)
    rows = _dot(sel_t.astype(BF16), hb)
    z_ref[...] = _chunks_to_storage(rows)

    sub_l = lax.broadcasted_iota(I32, (LANES, tm), 0)
    pg_t = jnp.zeros((LANES, tm), F32)
    for k in range(TOP_K):
        pg_t = jnp.where(sub_l == k, pos_rows[k], pg_t)
        pg_t = jnp.where(sub_l == TOP_K + k, gates[k], pg_t)
    pg_ref[...] = pg_t.T

    diag = (lax.broadcasted_iota(I32, (N_EXPERTS, LANES), 0) == lax.broadcasted_iota(I32, (N_EXPERTS, LANES), 1))
    cnt_row = jnp.sum(jnp.where(diag, chunks_c, 0.0), axis=0, keepdims=True)
    first_row = jnp.sum(jnp.where(diag, first_c, 0.0), axis=0, keepdims=True)
    sub8 = lax.broadcasted_iota(I32, (SUBLANES, LANES), 0)
    meta = jnp.where(sub8 == 0, cnt_row, jnp.where(sub8 == 1, first_row, 0.0))
    meta_ref[0] = meta.astype(I32)


def _moe_dispatch(x, mod_l, norm_g, w_router, b_router):
    wrt = jnp.pad(w_router.T, ((0, LANES - N_EXPERTS), (0, 0)))
    wrt_hi = wrt.astype(BF16)
    wrt_lo = (wrt - wrt_hi.astype(F32)).astype(BF16)
    brt = b_router.reshape(N_EXPERTS, 1)
    r = jnp.arange(TM_MOE)
    before = (r[:, None] < r[None, :]).astype(BF16)
    c = jnp.arange(LANES)
    below = (c[None, :] < c[:, None]).astype(BF16)
    const2 = lambda t: (0, 0)
    return pl.pallas_call(
        _dispatch_body,
        out_shape=[
            jax.ShapeDtypeStruct((N_TILE_MOE * LOCAL_CHUNKS, 2 * CHUNK, HALF), BF16),
            jax.ShapeDtypeStruct((T, LANES), F32),
            jax.ShapeDtypeStruct((N_TILE_MOE, SUBLANES, LANES), I32),
        ],
        grid=(N_TILE_MOE,),
        in_specs=[
            pl.BlockSpec((TM_MOE, D), lambda t: (t, 0)),
            pl.BlockSpec((None, 6, D), lambda t: (_cond_row(t, TM_MOE), 0, 0)),
            pl.BlockSpec((1, D), const2),
            pl.BlockSpec((LANES, D), const2),
            pl.BlockSpec((LANES, D), const2),
            pl.BlockSpec((N_EXPERTS, 1), const2),
            pl.BlockSpec((TM_MOE, TM_MOE), const2),
            pl.BlockSpec((LANES, LANES), const2),
        ],
        out_specs=[
            pl.BlockSpec((LOCAL_CHUNKS, 2 * CHUNK, HALF), lambda t: (t, 0, 0)),
            pl.BlockSpec((TM_MOE, LANES), lambda t: (t, 0)),
            pl.BlockSpec((1, SUBLANES, LANES), lambda t: (t, 0, 0)),
        ],
        compiler_params=_cparams(("arbitrary",)),
        name="moe_dispatch",
    )(x, mod_l, norm_g.reshape(1, D), wrt_hi, wrt_lo, brt, before, below)


def _expert_body(cnt_ref, first_ref, tstart_ref, ntile_ref, z_hbm, wgu_ref, bgu_ref, wd_ref, bd_ref, perm_ref,
                 y_hbm, wgu_s, wd_s, xbuf, ybuf, src_tab, sem_in, sem_out):
    e = pl.program_id(0)
    t0 = tstart_ref[e]
    nt = ntile_ref[e]
    total = tstart_ref[N_EXPERTS - 1] + ntile_ref[N_EXPERTS - 1]

    def build_source_table():
        def clear(p, carry):
            src_tab[p] = 0
            return carry

        def per_expert(ex, carry):
            def per_token_tile(tt, pos):
                n = cnt_ref[tt * N_EXPERTS + ex]
                first = tt * LOCAL_CHUNKS + first_ref[tt * N_EXPERTS + ex]
                for c in range(TABLE_RUN):
                    src_tab[pos + c] = first + c

                def per_chunk(c, inner):
                    src_tab[pos + c] = first + c
                    return inner

                lax.fori_loop(TABLE_RUN, n, per_chunk, 0)
                return pos + n

            pos = lax.fori_loop(0, N_TILE_MOE, per_token_tile, tstart_ref[ex] * EXP_CHUNKS)
            lax.fori_loop(pos, (tstart_ref[ex] + ntile_ref[ex]) * EXP_CHUNKS, clear, 0)
            return carry

        lax.fori_loop(0, N_EXPERTS, per_expert, 0)

    def start_gather(g, slot):
        base = jnp.minimum(g, total - 1) * EXP_CHUNKS
        for c in range(EXP_CHUNKS):
            pltpu.make_async_copy(z_hbm.at[src_tab[base + c]], xbuf.at[slot, c], sem_in.at[slot]).start(priority=1)

    def wait_gather(slot):
        pltpu.make_async_copy(z_hbm.at[pl.ds(0, EXP_CHUNKS)], xbuf.at[slot], sem_in.at[slot]).wait()

    def out_copy(g, slot):
        return pltpu.make_async_copy(ybuf.at[slot], y_hbm.at[pl.ds(g * EXP_CHUNKS, EXP_CHUNKS)],
                                     sem_out.at[slot])

    @pl.when(e == 0)
    def _():
        build_source_table()
        for g in range(N_XBUF - 1):
            start_gather(g, g)

    @pl.when(nt > 0)
    def _():
        for blk in range(2 * D_EXPERT // MXU_DIM):
            cols = slice(blk * MXU_DIM, (blk + 1) * MXU_DIM)
            wgu_s[:, cols] = _dot(wgu_ref[:, cols].astype(BF16), perm_ref[...]).astype(BF16)
        wd_s[...] = wd_ref[...].astype(BF16)

    def tile_body(i, carry):
        g = t0 + i
        slot = lax.rem(g, N_XBUF)
        oslot = lax.rem(g, 2)
        wait_gather(slot)
        x = _chunks_from_storage(xbuf[slot])
        acts = []
        for blk in range(2 * D_EXPERT // MXU_DIM):
            cols = slice(blk * MXU_DIM, (blk + 1) * MXU_DIM)
            gu = _dot(x, wgu_s[:, cols]) + bgu_ref[:, cols]
            x_glu = jnp.minimum(gu[:, :LANES], SWIGLU_LIMIT)
            x_lin = jnp.clip(gu[:, LANES:], -SWIGLU_LIMIT, SWIGLU_LIMIT)
            acts.append((x_glu * jax.nn.sigmoid(SWIGLU_ALPHA * x_glu) * (x_lin + 1.0)).astype(BF16))
        y = _dot(jnp.concatenate(acts, axis=1), wd_s[...]) + bd_ref[...]
        start_gather(g + N_XBUF - 1, lax.rem(g + N_XBUF - 1, N_XBUF))

        @pl.when(g >= 2)
        def _():
            out_copy(g - 2, oslot).wait()

        ybuf[oslot] = _chunks_to_storage(y)
        out_copy(g, oslot).start()
        return carry

    lax.fori_loop(0, nt, tile_body, 0)

    @pl.when(e == N_EXPERTS - 1)
    def _():
        for ahead in range(N_XBUF - 1):
            wait_gather(lax.rem(total + ahead, N_XBUF))
        last = total - 1
        out_copy(last, lax.rem(last, 2)).wait()

        @pl.when(total >= 2)
        def _():
            out_copy(last - 1, lax.rem(last - 1, 2)).wait()

        ybuf[0] = jnp.zeros((EXP_CHUNKS, 2 * CHUNK, HALF), BF16)

        def start_fill(g, carry):
            out_copy(g, 0).start()
            return carry

        def wait_fill(g, carry):
            out_copy(g, 0).wait()
            return carry

        lax.fori_loop(total, MAX_EXP_TILES, start_fill, 0)
        lax.fori_loop(total, MAX_EXP_TILES, wait_fill, 0)


def _moe_experts(z, cnt, first, tstart, ntile, layer, w_gate_up, bgu_all, w_down, b_down):
    n_gu = 2 * D_EXPERT
    i = jnp.arange(MXU_DIM)
    perm = (i[:, None] == jnp.where(i < LANES, 2 * i, 2 * (i - LANES) + 1)[None, :]).astype(BF16)
    wmap = lambda e, *_: (layer, e, 0, 0)
    grid_spec = pltpu.PrefetchScalarGridSpec(
        num_scalar_prefetch=4,
        grid=(N_EXPERTS,),
        in_specs=[
            pl.BlockSpec(memory_space=pl.ANY),
            pl.BlockSpec((None, None, D, n_gu), wmap),
            pl.BlockSpec((None, None, 1, n_gu), wmap),
            pl.BlockSpec((None, None, D_EXPERT, D), wmap),
            pl.BlockSpec((None, None, 1, D), wmap),
            pl.BlockSpec((MXU_DIM, MXU_DIM), lambda e, *_: (0, 0)),
        ],
        out_specs=pl.BlockSpec(memory_space=pl.ANY),
        scratch_shapes=[
            pltpu.VMEM((D, n_gu), BF16),
            pltpu.VMEM((D_EXPERT, D), BF16),
            pltpu.VMEM((N_XBUF, EXP_CHUNKS, 2 * CHUNK, HALF), BF16),
            pltpu.VMEM((2, EXP_CHUNKS, 2 * CHUNK, HALF), BF16),
            pltpu.SMEM((MAX_EXP_CHUNKS + TABLE_RUN,), I32),
            pltpu.SemaphoreType.DMA((N_XBUF,)),
            pltpu.SemaphoreType.DMA((2,)),
        ],
    )
    return pl.pallas_call(
        _expert_body,
        out_shape=jax.ShapeDtypeStruct((MAX_EXP_CHUNKS, 2 * CHUNK, HALF), BF16),
        grid_spec=grid_spec,
        compiler_params=_cparams(("arbitrary",)),
        name="moe_experts",
    )(cnt, first, tstart, ntile, z, w_gate_up, bgu_all, w_down,
      b_down.reshape(DEPTH, N_EXPERTS, 1, D), perm)


def _combine_body(src_ref, x_ref, mod_ref, pg_ref, y_hbm, *rest, split):
    o_refs, (ybuf, sem) = rest[:-2], rest[-2:]
    t = pl.program_id(0)
    n_t = pl.num_programs(0)
    slot = lax.rem(t, N_YBUF)

    def start_gather(tile, sl):
        base = jnp.minimum(tile, n_t - 1) * LOCAL_CHUNKS
        for c in range(LOCAL_CHUNKS):
            pltpu.make_async_copy(y_hbm.at[src_ref[base + c]], ybuf.at[sl, c], sem.at[sl]).start(priority=c % 2)

    def wait_gather(sl):
        pltpu.make_async_copy(y_hbm.at[pl.ds(0, LOCAL_CHUNKS)], ybuf.at[sl], sem.at[sl]).wait()

    @pl.when(t == 0)
    def _():
        for ahead in range(N_YBUF - 1):
            start_gather(ahead, ahead)

    wait_gather(slot)
    y = _chunks_from_storage(ybuf[slot])
    pg = pg_ref[...]
    col = lax.broadcasted_iota(I32, (TM_MOE, LOCAL_ROWS), 1).astype(F32)
    comb = jnp.zeros((TM_MOE, LOCAL_ROWS), F32)
    for k in range(TOP_K):
        comb = jnp.where(col == pg[:, k:k + 1], pg[:, TOP_K + k:TOP_K + k + 1], comb)
    moe = _dot(comb.astype(BF16), y)
    ahead_tile = t + N_YBUF - 1
    start_gather(ahead_tile, lax.rem(ahead_tile, N_YBUF))
    new_x = x_ref[...] + mod_ref[5:6, :] * moe
    if split:
        ctx_tiles = T_CTX // TM_MOE

        @pl.when(t < ctx_tiles)
        def _():
            o_refs[0][...] = new_x

        @pl.when(t >= ctx_tiles)
        def _():
            o_refs[1][...] = new_x
    else:
        o_refs[0][...] = new_x

    @pl.when(t == n_t - 1)
    def _():
        for ahead in range(1, N_YBUF):
            wait_gather(lax.rem(t + ahead, N_YBUF))


def _moe_combine(x, mod_l, pg, y, src, *, split):
    ctx_tiles = T_CTX // TM_MOE
    if split:
        out_shape = [jax.ShapeDtypeStruct((T_CTX, D), F32), jax.ShapeDtypeStruct((T_LAT, D), F32)]
        out_specs = [
            pl.BlockSpec((TM_MOE, D), lambda t, *_: (jnp.minimum(t, ctx_tiles - 1), 0)),
            pl.BlockSpec((TM_MOE, D), lambda t, *_: (jnp.maximum(t - ctx_tiles, 0), 0)),
        ]
        aliases = {}
    else:
        out_shape = jax.ShapeDtypeStruct((T, D), F32)
        out_specs = pl.BlockSpec((TM_MOE, D), lambda t, *_: (t, 0))
        aliases = {1: 0}
    grid_spec = pltpu.PrefetchScalarGridSpec(
        num_scalar_prefetch=1,
        grid=(N_TILE_MOE,),
        in_specs=[
            pl.BlockSpec((TM_MOE, D), lambda t, *_: (t, 0)),
            pl.BlockSpec((None, 6, D), lambda t, *_: (_cond_row(t, TM_MOE), 0, 0)),
            pl.BlockSpec((TM_MOE, LANES), lambda t, *_: (t, 0)),
            pl.BlockSpec(memory_space=pl.ANY),
        ],
        out_specs=out_specs,
        scratch_shapes=[
            pltpu.VMEM((N_YBUF, LOCAL_CHUNKS, 2 * CHUNK, HALF), BF16),
            pltpu.SemaphoreType.DMA((N_YBUF,)),
        ],
    )
    return pl.pallas_call(
        functools.partial(_combine_body, split=split),
        out_shape=out_shape,
        grid_spec=grid_spec,
        input_output_aliases=aliases,
        compiler_params=_cparams(("arbitrary",)),
        name="moe_combine_final" if split else "moe_combine",
    )(src, x, mod_l, pg, y)


def _moe_layer(x, mod_l, norm_g, layer, w_router, b_router, w_gate_up, bgu_all, w_down, b_down, *, split):
    z, pg, meta = _moe_dispatch(x, mod_l, norm_g, w_router, b_router)
    cnt = meta[:, 0, :N_EXPERTS]
    first = meta[:, 1, :N_EXPERTS]
    ntile = (jnp.sum(cnt, axis=0) + EXP_CHUNKS - 1) // EXP_CHUNKS
    tstart = jnp.cumsum(ntile) - ntile
    y = _moe_experts(z, cnt.reshape(-1), first.reshape(-1), tstart, ntile, layer,
                     w_gate_up, bgu_all, w_down, b_down)
    end = first + cnt
    seen = jnp.cumsum(cnt, axis=0) - cnt
    j = jnp.arange(LOCAL_CHUNKS, dtype=I32)
    e_of_j = jnp.sum((end[:, None, :] <= j[None, :, None]).astype(I32), axis=-1)
    used = j[None, :] < end[:, -1:]
    own = jnp.arange(N_EXPERTS, dtype=I32)[None, None, :] == e_of_j[:, :, None]
    offset = tstart[None, :] * EXP_CHUNKS + seen - first
    src = j[None, :] + jnp.sum(jnp.where(own, offset[:, None, :], 0), axis=-1)
    src = jnp.where(used, src, 0).astype(I32).reshape(-1)
    return _moe_combine(x, mod_l, pg, y, src, split=split)


def kernel(x_prompt, x_sample, c, cache_swa_k, cache_swa_v, cache_diff_k, cache_diff_v, c_ctx, norm_mix_g, norm_ffn_g, ada_w, ada_b, conv_w_in, conv_w, conv_w_out, swa_w_qkv, swa_q_gain, swa_k_gain, swa_sink, swa_w_out, diff_w_qkv, diff_q_gain, diff_k_gain, diff_lam_q1, diff_lam_k1, diff_lam_q2, diff_lam_k2, diff_sub_gain, diff_w_out, moe_w_router, moe_b_router, moe_w_gate_up, moe_b_gate_up, moe_w_down, moe_b_down):
    x = (x_prompt.reshape(T_CTX, D), x_sample.reshape(T_LAT, D))
    cond = jnp.concatenate([c_ctx[None, :], c, jnp.zeros((COND_PAD - N_COND, D), F32)], axis=0)
    mod = _modulation(cond, ada_w, ada_b).reshape(DEPTH, COND_PAD, 6, D)
    rope_tabs = _rope_tables()
    n_gu = 2 * D_EXPERT
    bgu_all = moe_b_gate_up.reshape(DEPTH, N_EXPERTS, n_gu // MXU_DIM, LANES, 2)
    bgu_all = bgu_all.transpose(0, 1, 2, 4, 3).reshape(DEPTH, N_EXPERTS, 1, n_gu)
    new_kv = {}
    for i in range(DEPTH):
        j = i // 3
        kind = i % 3
        mod_l = mod[i]
        if kind == 0:
            x = _conv_layer(x, mod_l, norm_mix_g[i], conv_w_in[j].astype(BF16), conv_w[j],
                            conv_w_out[j].astype(BF16))
        elif kind == 1:
            w = swa_w_qkv[j].astype(BF16)
            w_out = swa_w_out[j].astype(BF16)
            nkv = SWA_KV_HEADS * HEAD_DIM
            dims = dict(nq=SWA_Q_HEADS * HEAD_DIM // LANES, nk=nkv // LANES, nv=nkv)
            q, k, v, kf, vf = _qkv_proj(x, mod_l, norm_mix_g[i], w, swa_q_gain[j], swa_k_gain[j], None,
                                        latent=False, **dims)
            new_kv["swa_k"] = kf.reshape(BATCH, 1, SEQ, SWA_KV_HEADS, HEAD_DIM)
            new_kv["swa_v"] = vf.reshape(BATCH, 1, SEQ, SWA_KV_HEADS, HEAD_DIM)
            x = _swa_ctx_attention(x, mod_l, q, k, v, swa_sink[j], w_out)
            q, k, v = _qkv_proj(x, mod_l, norm_mix_g[i], w, swa_q_gain[j], swa_k_gain[j], rope_tabs,
                                latent=True, **dims)
            kc = cache_swa_k[:, j].reshape(DEC_BATCH, PAST_LEN, nkv).astype(BF16)
            vc = cache_swa_v[:, j].reshape(DEC_BATCH, PAST_LEN, nkv).astype(BF16)
            x = _swa_lat_attention(x, mod_l, q, k, v, kc, vc, swa_sink[j], w_out)
        else:
            w = diff_w_qkv[j].astype(BF16)
            w_out = diff_w_out[j].astype(BF16)
            lam_init = 0.8 - 0.6 * math.exp(-0.3 * i)
            lam_vecs = jnp.stack([diff_lam_q1[j], diff_lam_k1[j], diff_lam_q2[j], diff_lam_k2[j]])
            sub_gain = diff_sub_gain[j].reshape(1, 2 * HEAD_DIM)
            dims = dict(nq=D // LANES, nk=D // LANES, nv=D)
            q, k, v, kf, vf = _qkv_proj(x, mod_l, norm_mix_g[i], w, diff_q_gain[j], diff_k_gain[j], None,
                                        latent=False, **dims)
            new_kv["diff_k"] = kf.reshape(BATCH, 1, SEQ, DIFF_HEADS, 2, HEAD_DIM)
            new_kv["diff_v"] = vf.reshape(BATCH, 1, SEQ, DIFF_HEADS, 2 * HEAD_DIM)
            x = _diff_attention(x, mod_l, q, k.reshape(BATCH, SEQ, D), v.reshape(BATCH, SEQ, D), None,
                                lam_vecs, sub_gain, w_out, latent=False, lam_init=lam_init)
            q, k, v = _qkv_proj(x, mod_l, norm_mix_g[i], w, diff_q_gain[j], diff_k_gain[j], rope_tabs,
                                latent=True, **dims)
            kc = cache_diff_k[:, j].reshape(DEC_BATCH, PAST_LEN, D).astype(BF16)
            vc = cache_diff_v[:, j].reshape(DEC_BATCH, PAST_LEN, D).astype(BF16)
            x = _diff_attention(x, mod_l, q, k.reshape(DEC_BATCH, DEC_SEQ, D), v.reshape(DEC_BATCH, DEC_SEQ, D),
                                (kc, vc), lam_vecs, sub_gain, w_out, latent=True, lam_init=lam_init)
        x = _moe_layer(x, mod_l, norm_ffn_g[i], i, moe_w_router[i], moe_b_router[i], moe_w_gate_up,
                       bgu_all, moe_w_down, moe_b_down, split=i == DEPTH - 1)
    y_prompt, y_sample = x
    return (y_prompt.reshape(BATCH, SEQ, D), y_sample.reshape(DEC_BATCH, DEC_SEQ, D),
            new_kv["swa_k"], new_kv["swa_v"], new_kv["diff_k"], new_kv["diff_v"])
```

```python
import functools
import math

import jax
import jax.numpy as jnp
from jax import lax
from jax.experimental import pallas as pl
from jax.experimental.pallas import tpu as pltpu

F32 = jnp.float32
BF16 = jnp.bfloat16
I32 = jnp.int32

D = 1024
BATCH = 32
SEQ = 256
DEC_BATCH = 4
DEC_SEQ = 2048
PAST_LEN = 512
DEPTH = 4
GRID_W = 64
HEAD_DIM = 64
SWA_Q_HEADS = 16
SWA_KV_HEADS = 4
SWA_GROUP = SWA_Q_HEADS // SWA_KV_HEADS
SWA_WINDOW = 128
DIFF_HEADS = 8
N_EXPERTS = 32
TOP_K = 4
D_EXPERT = 1024
SWIGLU_ALPHA = 1.702
SWIGLU_LIMIT = 7.0
ROPE_BASE = 10000.0
EPS = 1e-6

T_CTX = BATCH * SEQ
T_LAT = DEC_BATCH * DEC_SEQ
T = T_CTX + T_LAT
N_COND = 1 + DEC_BATCH
COND_PAD = 8

LANES = 128
SUBLANES = 8
BF16_ROWS = 16
MXU_DIM = 256

TM_CONV = 512
TM_QKV = 512
TM_MOE = 256
TM_EXP = 512
CHUNK = SUBLANES
HALF = D // 2
assert 2 * CHUNK == BF16_ROWS
N_TILE_MOE = T // TM_MOE
LOCAL_ROWS = TM_MOE * TOP_K + N_EXPERTS * (CHUNK - 1)
LOCAL_ROWS = -(-LOCAL_ROWS // LANES) * LANES
LOCAL_CHUNKS = LOCAL_ROWS // CHUNK
EXP_CHUNKS = TM_EXP // CHUNK
MAX_GROUP_ROWS = T * TOP_K + N_TILE_MOE * N_EXPERTS * (CHUNK - 1)
MAX_EXP_TILES = -(-MAX_GROUP_ROWS // TM_EXP) + N_EXPERTS
MAX_EXP_CHUNKS = MAX_EXP_TILES * EXP_CHUNKS
N_XBUF = 3
N_YBUF = 3
TABLE_RUN = 8


def _chunks_to_storage(rows):
    r3 = rows.reshape(rows.shape[0] // CHUNK, CHUNK, D)
    return jnp.concatenate([r3[:, :, :HALF], r3[:, :, HALF:]], axis=1).astype(BF16)


def _chunks_from_storage(stored):
    s = stored.astype(F32)
    r3 = jnp.concatenate([s[:, :CHUNK, :], s[:, CHUNK:, :]], axis=-1)
    return r3.reshape(stored.shape[0] * CHUNK, D).astype(BF16)

VMEM_LIMIT = 56 * 1024 * 1024


def _cparams(sem, vmem=VMEM_LIMIT):
    return pltpu.CompilerParams(dimension_semantics=sem, vmem_limit_bytes=vmem)


def _dot(a, b):
    return jnp.dot(a, b, preferred_element_type=F32)


def _dot_nt(a, b):
    return lax.dot_general(a, b, (((1,), (1,)), ((), ())), preferred_element_type=F32)


def _dot_tn(a, b):
    return lax.dot_general(a, b, (((0,), (0,)), ((), ())), preferred_element_type=F32)


def _norm_mod(x, g, scale, shift):
    ms = jnp.mean(x * x, axis=-1, keepdims=True)
    y = x * lax.rsqrt(ms + EPS)
    return (y * g) * (1.0 + scale) + shift


MOD_TN = 1536


def _mod_body(cond_ref, w_ref, b_ref, o_ref):
    c = cond_ref[...]
    s = (c * jax.nn.sigmoid(c)).astype(BF16)
    o_ref[0] = _dot(s, w_ref[0].astype(BF16)) + b_ref[0]


def _modulation(cond, ada_w, ada_b):
    n_layer = ada_w.shape[0]
    n_out = ada_w.shape[2]
    return pl.pallas_call(
        _mod_body,
        out_shape=jax.ShapeDtypeStruct((n_layer, COND_PAD, n_out), F32),
        grid=(n_layer, n_out // MOD_TN),
        in_specs=[
            pl.BlockSpec((COND_PAD, D), lambda i, j: (0, 0)),
            pl.BlockSpec((1, D, MOD_TN), lambda i, j: (i, 0, j)),
            pl.BlockSpec((1, 1, MOD_TN), lambda i, j: (i, 0, j)),
        ],
        out_specs=pl.BlockSpec((1, COND_PAD, MOD_TN), lambda i, j: (i, 0, j)),
        compiler_params=_cparams(("arbitrary", "arbitrary")),
        name="modulation",
    )(cond, ada_w, ada_b.reshape(n_layer, 1, n_out))


def _cond_row(row_tile, tile_rows):
    ctx_tiles = T_CTX // tile_rows
    per_batch = DEC_SEQ // tile_rows
    return jnp.where(row_tile < ctx_tiles, 0, 1 + (row_tile - ctx_tiles) // per_batch)


def _conv_body(*refs, split):
    t = pl.program_id(0)
    if split:
        xc_ref, xl_ref, xcp_ref, xlp_ref, xcn_ref, xln_ref = refs[:6]
        refs = refs[6:]
        is_ctx = t < T_CTX // xc_ref.shape[0]
        x = jnp.where(is_ctx, xc_ref[...], xl_ref[...])
        x_before = jnp.where(is_ctx, xcp_ref[...], xlp_ref[...])
        x_after = jnp.where(is_ctx, xcn_ref[...], xln_ref[...])
    else:
        x_ref, xp_ref, xn_ref = refs[:3]
        refs = refs[3:]
        x, x_before, x_after = x_ref[...], xp_ref[...], xn_ref[...]
    mod_ref, g_ref, win_ref, cw_ref, wout_ref, o_ref = refs
    tm = x.shape[0]
    m = mod_ref[...]
    shift, scale, gate = m[0:1], m[1:2], m[2:3]
    x_ext = jnp.concatenate([x_before, x, x_after], axis=0)
    h = _norm_mod(x_ext, g_ref[...], scale, shift).astype(BF16)
    proj = _dot(h, win_ref[...])
    gate_b = proj[SUBLANES:SUBLANES + tm, :D]
    u_ext = proj[:, D:2 * D] * proj[:, 2 * D:]
    n_ext = tm + 2 * SUBLANES
    u = u_ext[SUBLANES:SUBLANES + tm]
    u_prev = pltpu.roll(u_ext, 1, axis=0)[SUBLANES:SUBLANES + tm]
    u_next = pltpu.roll(u_ext, n_ext - 1, axis=0)[SUBLANES:SUBLANES + tm]
    row = lax.broadcasted_iota(I32, (tm, 1), 0)
    grow = t * tm + row
    seq_mask = jnp.where(grow < T_CTX, SEQ - 1, DEC_SEQ - 1)
    pos = grow & seq_mask
    u_prev = jnp.where(pos == 0, 0.0, u_prev)
    u_next = jnp.where(pos == seq_mask, 0.0, u_next)
    cw = cw_ref[...]
    conv = u_prev * cw[0:1] + u * cw[1:2] + u_next * cw[2:3]
    y = (gate_b * conv).astype(BF16)
    o_ref[...] = x + gate * _dot(y, wout_ref[...])


def _conv_layer(xs, mod_l, norm_g, w_in, conv_w, w_out):
    tm = TM_CONV
    n_tiles = T // tm
    sub_per_tile = tm // SUBLANES
    split = isinstance(xs, tuple)
    if split:
        x_specs, x_args = [], []
        for kind in ("tile", "before", "after"):
            for x_part, tile0 in zip(xs, (0, T_CTX // tm)):
                part_tiles = x_part.shape[0] // tm
                last_sub = x_part.shape[0] // SUBLANES - 1
                if kind == "tile":
                    spec = pl.BlockSpec((tm, D), functools.partial(
                        lambda t, tile0, part_tiles: (jnp.clip(t - tile0, 0, part_tiles - 1), 0),
                        tile0=tile0, part_tiles=part_tiles))
                else:
                    shift = -1 if kind == "before" else sub_per_tile
                    spec = pl.BlockSpec((SUBLANES, D), functools.partial(
                        lambda t, tile0, last_sub, shift: (jnp.clip((t - tile0) * sub_per_tile + shift, 0, last_sub), 0),
                        tile0=tile0, last_sub=last_sub, shift=shift))
                x_specs.append(spec)
                x_args.append(x_part)
    else:
        last_sub = T // SUBLANES - 1
        x_specs = [
            pl.BlockSpec((tm, D), lambda t: (t, 0)),
            pl.BlockSpec((SUBLANES, D), lambda t: (jnp.maximum(t * sub_per_tile - 1, 0), 0)),
            pl.BlockSpec((SUBLANES, D), lambda t: (jnp.minimum((t + 1) * sub_per_tile, last_sub), 0)),
        ]
        x_args = [xs, xs, xs]
    return pl.pallas_call(
        functools.partial(_conv_body, split=split),
        out_shape=jax.ShapeDtypeStruct((T, D), F32),
        grid=(n_tiles,),
        in_specs=x_specs + [
            pl.BlockSpec((None, 6, D), lambda t: (_cond_row(t, tm), 0, 0)),
            pl.BlockSpec((1, D), lambda t: (0, 0)),
            pl.BlockSpec((D, 3 * D), lambda t: (0, 0)),
            pl.BlockSpec((3, D), lambda t: (0, 0)),
            pl.BlockSpec((D, D), lambda t: (0, 0)),
        ],
        out_specs=pl.BlockSpec((tm, D), lambda t: (t, 0)),
        compiler_params=_cparams(("arbitrary",)),
        name="conv_mixer",
    )(*x_args, mod_l, norm_g.reshape(1, D), w_in, conv_w, w_out)


def _head_norm(blk, gain):
    lane = lax.broadcasted_iota(I32, blk.shape, 1)
    low = lane < HEAD_DIM
    s = blk * blk
    s_lo = jnp.sum(jnp.where(low, s, 0.0), axis=-1, keepdims=True)
    s_hi = jnp.sum(jnp.where(low, 0.0, s), axis=-1, keepdims=True)
    ms = jnp.where(low, s_lo, s_hi) * (1.0 / HEAD_DIM)
    return blk * lax.rsqrt(ms + EPS) * gain


def _rope(blk, cos, sin_signed):
    lane = lax.broadcasted_iota(I32, blk.shape, 1)
    first = (lane & 31) < 16
    partner = jnp.where(first, pltpu.roll(blk, LANES - 16, axis=1), pltpu.roll(blk, 16, axis=1))
    return blk * cos + partner * sin_signed


def _qkv_body(*refs, nq, nk, nv, rope, emit_f32, q_scale):
    refs = list(refs)
    x_ref, mod_ref, g_ref, w_ref, qg_ref, kg_ref = refs[:6]
    refs = refs[6:]
    if rope:
        cos_ref, sin_ref = refs[:2]
        refs = refs[2:]
    q_ref, k_ref, v_ref = refs[:3]
    refs = refs[3:]
    if emit_f32:
        kf_ref, vf_ref = refs[:2]
    x = x_ref[...]
    m = mod_ref[...]
    h = _norm_mod(x, g_ref[...], m[1:2], m[0:1]).astype(BF16)
    proj = _dot(h, w_ref[...])
    if rope:
        cos = cos_ref[...]
        sin = sin_ref[...]
    for j in range(nq):
        blk = _head_norm(proj[:, j * LANES:(j + 1) * LANES], qg_ref[...])
        if rope:
            blk = _rope(blk, cos, sin)
        q_ref[:, j * LANES:(j + 1) * LANES] = (blk * q_scale).astype(BF16)
    off = nq * LANES
    for j in range(nk):
        blk = _head_norm(proj[:, off + j * LANES:off + (j + 1) * LANES], kg_ref[...])
        if emit_f32:
            kf_ref[:, j * LANES:(j + 1) * LANES] = blk
        if rope:
            blk = _rope(blk, cos, sin)
        k_ref[:, j * LANES:(j + 1) * LANES] = blk.astype(BF16)
    off = (nq + nk) * LANES
    v = proj[:, off:off + nv]
    v_ref[...] = v.astype(BF16)
    if emit_f32:
        vf_ref[...] = v


def _qkv_proj(x, mod_l, norm_g, w, q_gain, k_gain, rope_tabs, *, latent, nq, nk, nv):
    tm = TM_QKV
    n_rows = T_LAT if latent else T_CTX
    n_tiles = n_rows // tm
    tile0 = (T_CTX // tm) if latent else 0
    n_proj = (nq + nk) * LANES + nv
    gain2 = lambda g: jnp.concatenate([g, g]).reshape(1, LANES)
    in_specs = [
        pl.BlockSpec((tm, D), lambda t: (t + tile0, 0)),
        pl.BlockSpec((None, 6, D), lambda t: (_cond_row(t + tile0, tm), 0, 0)),
        pl.BlockSpec((1, D), lambda t: (0, 0)),
        pl.BlockSpec((D, n_proj), lambda t: (0, 0)),
        pl.BlockSpec((1, LANES), lambda t: (0, 0)),
        pl.BlockSpec((1, LANES), lambda t: (0, 0)),
    ]
    args = [x, mod_l, norm_g.reshape(1, D), w, gain2(q_gain), gain2(k_gain)]
    if latent:
        per_seq = DEC_SEQ // tm
        in_specs += [pl.BlockSpec((tm, LANES), lambda t: (t % per_seq, 0))] * 2
        args += list(rope_tabs)
    out_shape = [
        jax.ShapeDtypeStruct((n_rows, nq * LANES), BF16),
        jax.ShapeDtypeStruct((n_rows, nk * LANES), BF16),
        jax.ShapeDtypeStruct((n_rows, nv), BF16),
    ]
    out_specs = [
        pl.BlockSpec((tm, nq * LANES), lambda t: (t, 0)),
        pl.BlockSpec((tm, nk * LANES), lambda t: (t, 0)),
        pl.BlockSpec((tm, nv), lambda t: (t, 0)),
    ]
    if not latent:
        out_shape += [jax.ShapeDtypeStruct((n_rows, nk * LANES), F32), jax.ShapeDtypeStruct((n_rows, nv), F32)]
        out_specs += [pl.BlockSpec((tm, nk * LANES), lambda t: (t, 0)), pl.BlockSpec((tm, nv), lambda t: (t, 0))]
    body = functools.partial(_qkv_body, nq=nq, nk=nk, nv=nv, rope=latent, emit_f32=not latent,
                             q_scale=HEAD_DIM ** -0.5)
    return pl.pallas_call(
        body,
        out_shape=out_shape,
        grid=(n_tiles,),
        in_specs=in_specs,
        out_specs=out_specs,
        compiler_params=_cparams(("arbitrary",)),
        name="qkv_latent" if latent else "qkv_context",
    )(*args)


def _rope_tables():
    rows = DEC_SEQ // GRID_W
    row_pos = jnp.repeat(jnp.arange(rows, dtype=F32), GRID_W)
    col_pos = jnp.tile(jnp.arange(GRID_W, dtype=F32), rows)
    half = HEAD_DIM // 2
    inv_freq = ROPE_BASE ** (-jnp.arange(0, half, 2, dtype=F32) / half)
    ang_r = row_pos[:, None] * inv_freq[None, :]
    ang_c = col_pos[:, None] * inv_freq[None, :]
    cr, sr, cc, sc = jnp.cos(ang_r), jnp.sin(ang_r), jnp.cos(ang_c), jnp.sin(ang_c)
    cos64 = jnp.concatenate([cr, cr, cc, cc], axis=-1)
    sin64 = jnp.concatenate([-sr, sr, -sc, sc], axis=-1)
    return jnp.concatenate([cos64, cos64], axis=-1), jnp.concatenate([sin64, sin64], axis=-1)


def _swap_halves(blk):
    return pltpu.roll(blk.astype(F32), HEAD_DIM, axis=1).astype(BF16)


def _kv_operands(kh, k_blk, v_blk):
    lane = lax.broadcasted_iota(I32, k_blk.shape, 1)
    low = lane < HEAD_DIM
    k_swp = _swap_halves(k_blk)
    if kh % 2 == 0:
        k2 = jnp.where(low, k_blk, k_swp)
        v_lo = v_blk
    else:
        k2 = jnp.where(low, k_swp, k_blk)
        v_lo = _swap_halves(v_blk)
    v2 = jnp.where(low, v_lo, jnp.ones_like(v_lo))
    return k2, v2


def _stack_query_heads(q_ref, kh, tq):
    lane = lax.broadcasted_iota(I32, (tq, LANES), 1)
    low = lane < HEAD_DIM
    parts = []
    for pair in range(SWA_GROUP // 2):
        blk = (SWA_GROUP // 2) * kh + pair
        qp = q_ref[:, blk * LANES:(blk + 1) * LANES]
        zero = jnp.zeros_like(qp)
        parts += [jnp.where(low, qp, zero), jnp.where(low, zero, qp)]
    return jnp.concatenate(parts, axis=0)


def _sink_column(sink_ref, kh, tq):
    row = lax.broadcasted_iota(I32, (SWA_GROUP * tq, 1), 0)
    col = jnp.full((SWA_GROUP * tq, 1), sink_ref[SWA_GROUP * kh], F32)
    for g in range(1, SWA_GROUP):
        col = jnp.where(row >= g * tq, sink_ref[SWA_GROUP * kh + g], col)
    return col


def _store_head_group(att_ref, kh, tq, acc, sink_term):
    lane = lax.broadcasted_iota(I32, (tq, LANES), 1)
    low = lane < HEAD_DIM
    den = acc[:, HEAD_DIM:HEAD_DIM + 1] + sink_term
    r = acc / den
    r_hi = pltpu.roll(r, HEAD_DIM, axis=1)
    for pair in range(SWA_GROUP // 2):
        blk = (SWA_GROUP // 2) * kh + pair
        a = r[(2 * pair) * tq:(2 * pair + 1) * tq]
        b = r_hi[(2 * pair + 1) * tq:(2 * pair + 2) * tq]
        att_ref[:, blk * LANES:(blk + 1) * LANES] = jnp.where(low, a, b).astype(BF16)


def _swa_ctx_body(sink_ref, x_ref, mod_ref, q_ref, k_ref, v_ref, wout_ref, o_ref, att_ref):
    tq = q_ref.shape[0]
    for kh in range(SWA_KV_HEADS):
        blk = kh // 2
        k2, v2 = _kv_operands(kh, k_ref[:, blk * LANES:(blk + 1) * LANES], v_ref[:, blk * LANES:(blk + 1) * LANES])
        qs = _stack_query_heads(q_ref, kh, tq)
        sink = _sink_column(sink_ref, kh, tq)
        s = _dot_nt(qs, k2)
        mx = jnp.maximum(jnp.max(s, axis=-1, keepdims=True), sink)
        p = jnp.exp(s - mx)
        acc = _dot(p.astype(BF16), v2)
        _store_head_group(att_ref, kh, tq, acc, jnp.exp(sink - mx))
    gate = mod_ref[2:3, :]
    o_ref[...] = x_ref[...] + gate * _dot(att_ref[...], wout_ref[...])


def _swa_ctx_attention(x, mod_l, q, k, v, sink, w_out):
    nkv = SWA_KV_HEADS * HEAD_DIM
    return pl.pallas_call(
        _swa_ctx_body,
        out_shape=jax.ShapeDtypeStruct((T, D), F32),
        grid=(BATCH,),
        in_specs=[
            pl.BlockSpec(memory_space=pltpu.SMEM),
            pl.BlockSpec((SEQ, D), lambda b: (b, 0)),
            pl.BlockSpec((None, 6, D), lambda b: (0, 0, 0)),
            pl.BlockSpec((SEQ, D), lambda b: (b, 0)),
            pl.BlockSpec((SEQ, nkv), lambda b: (b, 0)),
            pl.BlockSpec((SEQ, nkv), lambda b: (b, 0)),
            pl.BlockSpec((D, D), lambda b: (0, 0)),
        ],
        out_specs=pl.BlockSpec((SEQ, D), lambda b: (b, 0)),
        scratch_shapes=[pltpu.VMEM((SEQ, D), BF16)],
        input_output_aliases={1: 0},
        compiler_params=_cparams(("arbitrary",)),
        name="swa_context_attention",
    )(sink, x, mod_l, q, k, v, w_out)


SWA_TQ = 128
SWA_KWIN = 3 * SWA_TQ


def _swa_lat_body(sink_ref, x_ref, mod_ref, q_ref, k_ref, v_ref, kc_ref, vc_ref, wout_ref, o_ref, att_ref):
    j = pl.program_id(1)
    tq = SWA_TQ
    start = jnp.clip((j - 1) * tq, 0, DEC_SEQ - SWA_KWIN)
    start = pl.multiple_of(start, tq)
    row = lax.broadcasted_iota(I32, (SWA_GROUP * tq, 1), 0)
    q_pos = j * tq + (row & (tq - 1))
    k_pos = start + lax.broadcasted_iota(I32, (1, SWA_KWIN), 1)
    valid = jnp.abs(k_pos - q_pos) <= SWA_WINDOW
    for kh in range(SWA_KV_HEADS):
        blk = kh // 2
        lanes = slice(blk * LANES, (blk + 1) * LANES)
        k2, v2 = _kv_operands(kh, k_ref[pl.ds(start, SWA_KWIN), lanes], v_ref[pl.ds(start, SWA_KWIN), lanes])
        kc2, vc2 = _kv_operands(kh, kc_ref[:, lanes], vc_ref[:, lanes])
        qs = _stack_query_heads(q_ref, kh, tq)
        sink = _sink_column(sink_ref, kh, tq)
        s_lat = jnp.where(valid, _dot_nt(qs, k2), -jnp.inf)
        s_ctx = _dot_nt(qs, kc2)
        mx = jnp.maximum(jnp.max(s_lat, axis=-1, keepdims=True), jnp.max(s_ctx, axis=-1, keepdims=True))
        mx = jnp.maximum(mx, sink)
        p_lat = jnp.exp(s_lat - mx).astype(BF16)
        p_ctx = jnp.exp(s_ctx - mx).astype(BF16)
        acc = _dot(p_lat, v2) + _dot(p_ctx, vc2)
        _store_head_group(att_ref, kh, tq, acc, jnp.exp(sink - mx))
    gate = mod_ref[2:3, :]
    o_ref[...] = x_ref[...] + gate * _dot(att_ref[...], wout_ref[...])


def _swa_lat_attention(x, mod_l, q, k, v, k_cache, v_cache, sink, w_out):
    nkv = SWA_KV_HEADS * HEAD_DIM
    nb = DEC_SEQ // SWA_TQ
    x_tile0 = T_CTX // SWA_TQ
    return pl.pallas_call(
        _swa_lat_body,
        out_shape=jax.ShapeDtypeStruct((T, D), F32),
        grid=(DEC_BATCH, nb),
        in_specs=[
            pl.BlockSpec(memory_space=pltpu.SMEM),
            pl.BlockSpec((SWA_TQ, D), lambda b, j: (x_tile0 + b * nb + j, 0)),
            pl.BlockSpec((None, 6, D), lambda b, j: (1 + b, 0, 0)),
            pl.BlockSpec((SWA_TQ, D), lambda b, j: (b * nb + j, 0)),
            pl.BlockSpec((DEC_SEQ, nkv), lambda b, j: (b, 0)),
            pl.BlockSpec((DEC_SEQ, nkv), lambda b, j: (b, 0)),
            pl.BlockSpec((None, PAST_LEN, nkv), lambda b, j: (b, 0, 0)),
            pl.BlockSpec((None, PAST_LEN, nkv), lambda b, j: (b, 0, 0)),
            pl.BlockSpec((D, D), lambda b, j: (0, 0)),
        ],
        out_specs=pl.BlockSpec((SWA_TQ, D), lambda b, j: (x_tile0 + b * nb + j, 0)),
        scratch_shapes=[pltpu.VMEM((SWA_TQ, D), BF16)],
        input_output_aliases={1: 0},
        compiler_params=_cparams(("arbitrary", "arbitrary")),
        name="swa_latent_attention",
    )(sink, x, mod_l, q, k, v, k_cache, v_cache, w_out)


DIFF_TQ = 256


def _diff_body(x_ref, mod_ref, q_ref, k_ref, v_ref, *rest, lam_init, cached):
    if cached:
        kc_ref, vc_ref = rest[:2]
        rest = rest[2:]
    lam_ref, sg_ref, wout_ref, o_ref, att_ref = rest
    tq = q_ref.shape[0]
    lane = lax.broadcasted_iota(I32, (tq, LANES), 1)
    low = lane < HEAD_DIM
    lv = lam_ref[...]
    lam = (jnp.exp(jnp.sum(lv[0:1] * lv[1:2], axis=-1, keepdims=True))
           - jnp.exp(jnp.sum(lv[2:3] * lv[3:4], axis=-1, keepdims=True)) + lam_init)
    sub_gain = sg_ref[...]
    for hd in range(DIFF_HEADS):
        lanes = slice(hd * LANES, (hd + 1) * LANES)
        qh = q_ref[:, lanes]
        vh = v_ref[:, lanes]
        zero = jnp.zeros_like(qh)
        q2 = jnp.concatenate([jnp.where(low, qh, zero), jnp.where(low, zero, qh)], axis=0)
        s = _dot_nt(q2, k_ref[:, lanes])
        mx = jnp.max(s, axis=-1, keepdims=True)
        if cached:
            s_c = _dot_nt(q2, kc_ref[:, lanes])
            mx = jnp.maximum(mx, jnp.max(s_c, axis=-1, keepdims=True))
        acc = _dot(jnp.exp(s - mx).astype(BF16), jnp.concatenate([vh, jnp.ones_like(vh)], axis=1))
        if cached:
            vch = vc_ref[:, lanes]
            acc += _dot(jnp.exp(s_c - mx).astype(BF16), jnp.concatenate([vch, jnp.ones_like(vch)], axis=1))
        r = acc[:, :LANES] / acc[:, LANES:LANES + 1]
        o = r[:tq] - lam * r[tq:]
        ms = jnp.mean(o * o, axis=-1, keepdims=True)
        o = (o * lax.rsqrt(ms + EPS) * sub_gain) * (1.0 - lam_init)
        att_ref[:, lanes] = o.astype(BF16)
    gate = mod_ref[2:3, :]
    o_ref[...] = x_ref[...] + gate * _dot(att_ref[...], wout_ref[...])


def _diff_attention(x, mod_l, q, k, v, cache, lam_vecs, sub_gain, w_out, *, latent, lam_init):
    tq = DIFF_TQ
    n_keys = k.shape[1]
    kv_map = lambda b, j: (b, 0, 0)
    kv_specs = [pl.BlockSpec((None, n_keys, D), kv_map), pl.BlockSpec((None, n_keys, D), kv_map)]
    kv_args = [k, v]
    if cache is not None:
        kv_specs += [pl.BlockSpec((None, PAST_LEN, D), kv_map), pl.BlockSpec((None, PAST_LEN, D), kv_map)]
        kv_args += list(cache)
    if latent:
        per_batch = DEC_SEQ // tq
        grid = (DEC_BATCH, per_batch)
        x_tile0 = T_CTX // tq
        x_map = lambda b, j: (x_tile0 + b * per_batch + j, 0)
        q_map = lambda b, j: (b * per_batch + j, 0)
        mod_map = lambda b, j: (1 + b, 0, 0)
    else:
        per_batch = SEQ // tq
        grid = (BATCH, per_batch)
        x_map = lambda b, j: (b * per_batch + j, 0)
        q_map = x_map
        mod_map = lambda b, j: (0, 0, 0)
    const2 = lambda b, j: (0, 0)
    return pl.pallas_call(
        functools.partial(_diff_body, lam_init=lam_init, cached=cache is not None),
        out_shape=jax.ShapeDtypeStruct((T, D), F32),
        grid=grid,
        in_specs=[
            pl.BlockSpec((tq, D), x_map),
            pl.BlockSpec((None, 6, D), mod_map),
            pl.BlockSpec((tq, D), q_map),
            *kv_specs,
            pl.BlockSpec((4, HEAD_DIM), const2),
            pl.BlockSpec((1, 2 * HEAD_DIM), const2),
            pl.BlockSpec((D, D), const2),
        ],
        out_specs=pl.BlockSpec((tq, D), x_map),
        scratch_shapes=[pltpu.VMEM((tq, D), BF16)],
        input_output_aliases={0: 0},
        compiler_params=_cparams(("arbitrary", "arbitrary")),
        name="diff_latent_attention" if latent else "diff_context_attention",
    )(x, mod_l, q, *kv_args, lam_vecs, sub_gain, w_out)


def _dispatch_body(x_ref, mod_ref, g_ref, wrt_hi_ref, wrt_lo_ref, brt_ref, before_ref, below_ref,
                   z_ref, pg_ref, meta_ref):
    tm = TM_MOE
    m = mod_ref[...]
    h = _norm_mod(x_ref[...], g_ref[...], m[4:5], m[3:4])
    hb = h.astype(BF16)
    h_lo = (h - hb.astype(F32)).astype(BF16)
    wrt_hi = wrt_hi_ref[...]
    logits_t = _dot_nt(wrt_hi, hb) + _dot_nt(wrt_hi, h_lo) + _dot_nt(wrt_lo_ref[...], hb)
    work = logits_t[:N_EXPERTS] + brt_ref[...]
    sub = lax.broadcasted_iota(I32, (N_EXPERTS, tm), 0)
    picks, vals = [], []
    for _ in range(TOP_K):
        best = jnp.max(work, axis=0, keepdims=True)
        idx = jnp.min(jnp.where(work == best, sub, N_EXPERTS), axis=0, keepdims=True)
        hit = sub == idx
        work = jnp.where(hit, -jnp.inf, work)
        picks.append(hit)
        vals.append(best)
    exps = [jnp.exp(v - vals[0]) for v in vals]
    den = exps[0] + exps[1] + exps[2] + exps[3]
    gates = [e / den for e in exps]
    chosen_b = jnp.where(picks[0] | picks[1] | picks[2] | picks[3], 1.0, 0.0).astype(BF16)

    rank_t = _dot(chosen_b, before_ref[...])
    count_c = _dot(chosen_b, jnp.ones((tm, LANES), BF16))
    chunks_c = jnp.floor((count_c + (CHUNK - 1)) * (1.0 / CHUNK))
    chunks_pad = jnp.concatenate([chunks_c, jnp.zeros((LANES - N_EXPERTS, LANES), F32)], axis=0)
    first_c = _dot(below_ref[...], chunks_pad.astype(BF16))[:N_EXPERTS]
    pos_full_t = jnp.concatenate([first_c] * (tm // LANES), axis=1) * CHUNK + rank_t
    pos_rows = [jnp.sum(jnp.where(p, pos_full_t, 0.0), axis=0, keepdims=True) for p in picks]

    row_id = lax.broadcasted_iota(I32, (LOCAL_ROWS, tm), 0).astype(F32)
    sel_t = jnp.zeros((LOCAL_ROWS, tm), F32)
    for k in range(TOP_K):
        sel_t = jnp.where(row_id == pos_rows[k], 1.0, sel_t)
    rows = _dot(sel_t.astype(BF16), hb)
    z_ref[...] = _chunks_to_storage(rows)

    sub_l = lax.broadcasted_iota(I32, (LANES, tm), 0)
    pg_t = jnp.zeros((LANES, tm), F32)
    for k in range(TOP_K):
        pg_t = jnp.where(sub_l == k, pos_rows[k], pg_t)
        pg_t = jnp.where(sub_l == TOP_K + k, gates[k], pg_t)
    pg_ref[...] = pg_t.T

    diag = (lax.broadcasted_iota(I32, (N_EXPERTS, LANES), 0) == lax.broadcasted_iota(I32, (N_EXPERTS, LANES), 1))
    cnt_row = jnp.sum(jnp.where(diag, chunks_c, 0.0), axis=0, keepdims=True)
    first_row = jnp.sum(jnp.where(diag, first_c, 0.0), axis=0, keepdims=True)
    sub8 = lax.broadcasted_iota(I32, (SUBLANES, LANES), 0)
    meta = jnp.where(sub8 == 0, cnt_row, jnp.where(sub8 == 1, first_row, 0.0))
    meta_ref[0] = meta.astype(I32)


def _moe_dispatch(x, mod_l, norm_g, w_router, b_router):
    wrt = jnp.pad(w_router.T, ((0, LANES - N_EXPERTS), (0, 0)))
    wrt_hi = wrt.astype(BF16)
    wrt_lo = (wrt - wrt_hi.astype(F32)).astype(BF16)
    brt = b_router.reshape(N_EXPERTS, 1)
    r = jnp.arange(TM_MOE)
    before = (r[:, None] < r[None, :]).astype(BF16)
    c = jnp.arange(LANES)
    below = (c[None, :] < c[:, None]).astype(BF16)
    const2 = lambda t: (0, 0)
    return pl.pallas_call(
        _dispatch_body,
        out_shape=[
            jax.ShapeDtypeStruct((N_TILE_MOE * LOCAL_CHUNKS, 2 * CHUNK, HALF), BF16),
            jax.ShapeDtypeStruct((T, LANES), F32),
            jax.ShapeDtypeStruct((N_TILE_MOE, SUBLANES, LANES), I32),
        ],
        grid=(N_TILE_MOE,),
        in_specs=[
            pl.BlockSpec((TM_MOE, D), lambda t: (t, 0)),
            pl.BlockSpec((None, 6, D), lambda t: (_cond_row(t, TM_MOE), 0, 0)),
            pl.BlockSpec((1, D), const2),
            pl.BlockSpec((LANES, D), const2),
            pl.BlockSpec((LANES, D), const2),
            pl.BlockSpec((N_EXPERTS, 1), const2),
            pl.BlockSpec((TM_MOE, TM_MOE), const2),
            pl.BlockSpec((LANES, LANES), const2),
        ],
        out_specs=[
            pl.BlockSpec((LOCAL_CHUNKS, 2 * CHUNK, HALF), lambda t: (t, 0, 0)),
            pl.BlockSpec((TM_MOE, LANES), lambda t: (t, 0)),
            pl.BlockSpec((1, SUBLANES, LANES), lambda t: (t, 0, 0)),
        ],
        compiler_params=_cparams(("arbitrary",)),
        name="moe_dispatch",
    )(x, mod_l, norm_g.reshape(1, D), wrt_hi, wrt_lo, brt, before, below)


def _expert_body(cnt_ref, first_ref, tstart_ref, ntile_ref, z_hbm, wgu_ref, bgu_ref, wd_ref, bd_ref, perm_ref,
                 y_hbm, wgu_s, wd_s, xbuf, ybuf, src_tab, sem_in, sem_out):
    e = pl.program_id(0)
    t0 = tstart_ref[e]
    nt = ntile_ref[e]
    total = tstart_ref[N_EXPERTS - 1] + ntile_ref[N_EXPERTS - 1]

    def build_source_table():
        def clear(p, carry):
            src_tab[p] = 0
            return carry

        def per_expert(ex, carry):
            def per_token_tile(tt, pos):
                n = cnt_ref[tt * N_EXPERTS + ex]
                first = tt * LOCAL_CHUNKS + first_ref[tt * N_EXPERTS + ex]
                for c in range(TABLE_RUN):
                    src_tab[pos + c] = first + c

                def per_chunk(c, inner):
                    src_tab[pos + c] = first + c
                    return inner

                lax.fori_loop(TABLE_RUN, n, per_chunk, 0)
                return pos + n

            pos = lax.fori_loop(0, N_TILE_MOE, per_token_tile, tstart_ref[ex] * EXP_CHUNKS)
            lax.fori_loop(pos, (tstart_ref[ex] + ntile_ref[ex]) * EXP_CHUNKS, clear, 0)
            return carry

        lax.fori_loop(0, N_EXPERTS, per_expert, 0)

    def start_gather(g, slot):
        base = jnp.minimum(g, total - 1) * EXP_CHUNKS
        for c in range(EXP_CHUNKS):
            pltpu.make_async_copy(z_hbm.at[src_tab[base + c]], xbuf.at[slot, c], sem_in.at[slot]).start(priority=1)

    def wait_gather(slot):
        pltpu.make_async_copy(z_hbm.at[pl.ds(0, EXP_CHUNKS)], xbuf.at[slot], sem_in.at[slot]).wait()

    def out_copy(g, slot):
        return pltpu.make_async_copy(ybuf.at[slot], y_hbm.at[pl.ds(g * EXP_CHUNKS, EXP_CHUNKS)],
                                     sem_out.at[slot])

    @pl.when(e == 0)
    def _():
        build_source_table()
        for g in range(N_XBUF - 1):
            start_gather(g, g)

    @pl.when(nt > 0)
    def _():
        for blk in range(2 * D_EXPERT // MXU_DIM):
            cols = slice(blk * MXU_DIM, (blk + 1) * MXU_DIM)
            wgu_s[:, cols] = _dot(wgu_ref[:, cols].astype(BF16), perm_ref[...]).astype(BF16)
        wd_s[...] = wd_ref[...].astype(BF16)

    def tile_body(i, carry):
        g = t0 + i
        slot = lax.rem(g, N_XBUF)
        oslot = lax.rem(g, 2)
        wait_gather(slot)
        x = _chunks_from_storage(xbuf[slot])
        acts = []
        for blk in range(2 * D_EXPERT // MXU_DIM):
            cols = slice(blk * MXU_DIM, (blk + 1) * MXU_DIM)
            gu = _dot(x, wgu_s[:, cols]) + bgu_ref[:, cols]
            x_glu = jnp.minimum(gu[:, :LANES], SWIGLU_LIMIT)
            x_lin = jnp.clip(gu[:, LANES:], -SWIGLU_LIMIT, SWIGLU_LIMIT)
            acts.append((x_glu * jax.nn.sigmoid(SWIGLU_ALPHA * x_glu) * (x_lin + 1.0)).astype(BF16))
        y = _dot(jnp.concatenate(acts, axis=1), wd_s[...]) + bd_ref[...]
        start_gather(g + N_XBUF - 1, lax.rem(g + N_XBUF - 1, N_XBUF))

        @pl.when(g >= 2)
        def _():
            out_copy(g - 2, oslot).wait()

        ybuf[oslot] = _chunks_to_storage(y)
        out_copy(g, oslot).start()
        return carry

    lax.fori_loop(0, nt, tile_body, 0)

    @pl.when(e == N_EXPERTS - 1)
    def _():
        for ahead in range(N_XBUF - 1):
            wait_gather(lax.rem(total + ahead, N_XBUF))
        last = total - 1
        out_copy(last, lax.rem(last, 2)).wait()

        @pl.when(total >= 2)
        def _():
            out_copy(last - 1, lax.rem(last - 1, 2)).wait()

        ybuf[0] = jnp.zeros((EXP_CHUNKS, 2 * CHUNK, HALF), BF16)

        def start_fill(g, carry):
            out_copy(g, 0).start()
            return carry

        def wait_fill(g, carry):
            out_copy(g, 0).wait()
            return carry

        lax.fori_loop(total, MAX_EXP_TILES, start_fill, 0)
        lax.fori_loop(total, MAX_EXP_TILES, wait_fill, 0)


def _moe_experts(z, cnt, first, tstart, ntile, layer, w_gate_up, bgu_all, w_down, b_down):
    n_gu = 2 * D_EXPERT
    i = jnp.arange(MXU_DIM)
    perm = (i[:, None] == jnp.where(i < LANES, 2 * i, 2 * (i - LANES) + 1)[None, :]).astype(BF16)
    wmap = lambda e, *_: (layer, e, 0, 0)
    grid_spec = pltpu.PrefetchScalarGridSpec(
        num_scalar_prefetch=4,
        grid=(N_EXPERTS,),
        in_specs=[
            pl.BlockSpec(memory_space=pl.ANY),
            pl.BlockSpec((None, None, D, n_gu), wmap),
            pl.BlockSpec((None, None, 1, n_gu), wmap),
            pl.BlockSpec((None, None, D_EXPERT, D), wmap),
            pl.BlockSpec((None, None, 1, D), wmap),
            pl.BlockSpec((MXU_DIM, MXU_DIM), lambda e, *_: (0, 0)),
        ],
        out_specs=pl.BlockSpec(memory_space=pl.ANY),
        scratch_shapes=[
            pltpu.VMEM((D, n_gu), BF16),
            pltpu.VMEM((D_EXPERT, D), BF16),
            pltpu.VMEM((N_XBUF, EXP_CHUNKS, 2 * CHUNK, HALF), BF16),
            pltpu.VMEM((2, EXP_CHUNKS, 2 * CHUNK, HALF), BF16),
            pltpu.SMEM((MAX_EXP_CHUNKS + TABLE_RUN,), I32),
            pltpu.SemaphoreType.DMA((N_XBUF,)),
            pltpu.SemaphoreType.DMA((2,)),
        ],
    )
    return pl.pallas_call(
        _expert_body,
        out_shape=jax.ShapeDtypeStruct((MAX_EXP_CHUNKS, 2 * CHUNK, HALF), BF16),
        grid_spec=grid_spec,
        compiler_params=_cparams(("arbitrary",)),
        name="moe_experts",
    )(cnt, first, tstart, ntile, z, w_gate_up, bgu_all, w_down,
      b_down.reshape(DEPTH, N_EXPERTS, 1, D), perm)


def _combine_body(src_ref, x_ref, mod_ref, pg_ref, y_hbm, *rest, split):
    o_refs, (ybuf, sem) = rest[:-2], rest[-2:]
    t = pl.program_id(0)
    n_t = pl.num_programs(0)
    slot = lax.rem(t, N_YBUF)

    def start_gather(tile, sl):
        base = jnp.minimum(tile, n_t - 1) * LOCAL_CHUNKS
        for c in range(LOCAL_CHUNKS):
            pltpu.make_async_copy(y_hbm.at[src_ref[base + c]], ybuf.at[sl, c], sem.at[sl]).start(priority=c % 2)

    def wait_gather(sl):
        pltpu.make_async_copy(y_hbm.at[pl.ds(0, LOCAL_CHUNKS)], ybuf.at[sl], sem.at[sl]).wait()

    @pl.when(t == 0)
    def _():
        for ahead in range(N_YBUF - 1):
            start_gather(ahead, ahead)

    wait_gather(slot)
    y = _chunks_from_storage(ybuf[slot])
    pg = pg_ref[...]
    col = lax.broadcasted_iota(I32, (TM_MOE, LOCAL_ROWS), 1).astype(F32)
    comb = jnp.zeros((TM_MOE, LOCAL_ROWS), F32)
    for k in range(TOP_K):
        comb = jnp.where(col == pg[:, k:k + 1], pg[:, TOP_K + k:TOP_K + k + 1], comb)
    moe = _dot(comb.astype(BF16), y)
    ahead_tile = t + N_YBUF - 1
    start_gather(ahead_tile, lax.rem(ahead_tile, N_YBUF))
    new_x = x_ref[...] + mod_ref[5:6, :] * moe
    if split:
        ctx_tiles = T_CTX // TM_MOE

        @pl.when(t < ctx_tiles)
        def _():
            o_refs[0][...] = new_x

        @pl.when(t >= ctx_tiles)
        def _():
            o_refs[1][...] = new_x
    else:
        o_refs[0][...] = new_x

    @pl.when(t == n_t - 1)
    def _():
        for ahead in range(1, N_YBUF):
            wait_gather(lax.rem(t + ahead, N_YBUF))


def _moe_combine(x, mod_l, pg, y, src, *, split):
    ctx_tiles = T_CTX // TM_MOE
    if split:
        out_shape = [jax.ShapeDtypeStruct((T_CTX, D), F32), jax.ShapeDtypeStruct((T_LAT, D), F32)]
        out_specs = [
            pl.BlockSpec((TM_MOE, D), lambda t, *_: (jnp.minimum(t, ctx_tiles - 1), 0)),
            pl.BlockSpec((TM_MOE, D), lambda t, *_: (jnp.maximum(t - ctx_tiles, 0), 0)),
        ]
        aliases = {}
    else:
        out_shape = jax.ShapeDtypeStruct((T, D), F32)
        out_specs = pl.BlockSpec((TM_MOE, D), lambda t, *_: (t, 0))
        aliases = {1: 0}
    grid_spec = pltpu.PrefetchScalarGridSpec(
        num_scalar_prefetch=1,
        grid=(N_TILE_MOE,),
        in_specs=[
            pl.BlockSpec((TM_MOE, D), lambda t, *_: (t, 0)),
            pl.BlockSpec((None, 6, D), lambda t, *_: (_cond_row(t, TM_MOE), 0, 0)),
            pl.BlockSpec((TM_MOE, LANES), lambda t, *_: (t, 0)),
            pl.BlockSpec(memory_space=pl.ANY),
        ],
        out_specs=out_specs,
        scratch_shapes=[
            pltpu.VMEM((N_YBUF, LOCAL_CHUNKS, 2 * CHUNK, HALF), BF16),
            pltpu.SemaphoreType.DMA((N_YBUF,)),
        ],
    )
    return pl.pallas_call(
        functools.partial(_combine_body, split=split),
        out_shape=out_shape,
        grid_spec=grid_spec,
        input_output_aliases=aliases,
        compiler_params=_cparams(("arbitrary",)),
        name="moe_combine_final" if split else "moe_combine",
    )(src, x, mod_l, pg, y)


def _moe_layer(x, mod_l, norm_g, layer, w_router, b_router, w_gate_up, bgu_all, w_down, b_down, *, split):
    z, pg, meta = _moe_dispatch(x, mod_l, norm_g, w_router, b_router)
    cnt = meta[:, 0, :N_EXPERTS]
    first = meta[:, 1, :N_EXPERTS]
    ntile = (jnp.sum(cnt, axis=0) + EXP_CHUNKS - 1) // EXP_CHUNKS
    tstart = jnp.cumsum(ntile) - ntile
    y = _moe_experts(z, cnt.reshape(-1), first.reshape(-1), tstart, ntile, layer,
                     w_gate_up, bgu_all, w_down, b_down)
    end = first + cnt
    seen = jnp.cumsum(cnt, axis=0) - cnt
    j = jnp.arange(LOCAL_CHUNKS, dtype=I32)
    e_of_j = jnp.sum((end[:, None, :] <= j[None, :, None]).astype(I32), axis=-1)
    used = j[None, :] < end[:, -1:]
    own = jnp.arange(N_EXPERTS, dtype=I32)[None, None, :] == e_of_j[:, :, None]
    offset = tstart[None, :] * EXP_CHUNKS + seen - first
    src = j[None, :] + jnp.sum(jnp.where(own, offset[:, None, :], 0), axis=-1)
    src = jnp.where(used, src, 0).astype(I32).reshape(-1)
    return _moe_combine(x, mod_l, pg, y, src, split=split)


def kernel(x_prompt, x_sample, c, cache_swa_k, cache_swa_v, cache_diff_k, cache_diff_v, c_ctx, norm_mix_g, norm_ffn_g, ada_w, ada_b, conv_w_in, conv_w, conv_w_out, swa_w_qkv, swa_q_gain, swa_k_gain, swa_sink, swa_w_out, diff_w_qkv, diff_q_gain, diff_k_gain, diff_lam_q1, diff_lam_k1, diff_lam_q2, diff_lam_k2, diff_sub_gain, diff_w_out, moe_w_router, moe_b_router, moe_w_gate_up, moe_b_gate_up, moe_w_down, moe_b_down):
    x = (x_prompt.reshape(T_CTX, D), x_sample.reshape(T_LAT, D))
    cond = jnp.concatenate([c_ctx[None, :], c, jnp.zeros((COND_PAD - N_COND, D), F32)], axis=0)
    mod = _modulation(cond, ada_w, ada_b).reshape(DEPTH, COND_PAD, 6, D)
    rope_tabs = _rope_tables()
    n_gu = 2 * D_EXPERT
    bgu_all = moe_b_gate_up.reshape(DEPTH, N_EXPERTS, n_gu // MXU_DIM, LANES, 2)
    bgu_all = bgu_all.transpose(0, 1, 2, 4, 3).reshape(DEPTH, N_EXPERTS, 1, n_gu)
    new_kv = {}
    for i in range(DEPTH):
        j = i // 3
        kind = i % 3
        mod_l = mod[i]
        if kind == 0:
            x = _conv_layer(x, mod_l, norm_mix_g[i], conv_w_in[j].astype(BF16), conv_w[j],
                            conv_w_out[j].astype(BF16))
        elif kind == 1:
            w = swa_w_qkv[j].astype(BF16)
            w_out = swa_w_out[j].astype(BF16)
            nkv = SWA_KV_HEADS * HEAD_DIM
            dims = dict(nq=SWA_Q_HEADS * HEAD_DIM // LANES, nk=nkv // LANES, nv=nkv)
            q, k, v, kf, vf = _qkv_proj(x, mod_l, norm_mix_g[i], w, swa_q_gain[j], swa_k_gain[j], None,
                                        latent=False, **dims)
            new_kv["swa_k"] = kf.reshape(BATCH, 1, SEQ, SWA_KV_HEADS, HEAD_DIM)
            new_kv["swa_v"] = vf.reshape(BATCH, 1, SEQ, SWA_KV_HEADS, HEAD_DIM)
            x = _swa_ctx_attention(x, mod_l, q, k, v, swa_sink[j], w_out)
            q, k, v = _qkv_proj(x, mod_l, norm_mix_g[i], w, swa_q_gain[j], swa_k_gain[j], rope_tabs,
                                latent=True, **dims)
            kc = cache_swa_k[:, j].reshape(DEC_BATCH, PAST_LEN, nkv).astype(BF16)
            vc = cache_swa_v[:, j].reshape(DEC_BATCH, PAST_LEN, nkv).astype(BF16)
            x = _swa_lat_attention(x, mod_l, q, k, v, kc, vc, swa_sink[j], w_out)
        else:
            w = diff_w_qkv[j].astype(BF16)
            w_out = diff_w_out[j].astype(BF16)
            lam_init = 0.8 - 0.6 * math.exp(-0.3 * i)
            lam_vecs = jnp.stack([diff_lam_q1[j], diff_lam_k1[j], diff_lam_q2[j], diff_lam_k2[j]])
            sub_gain = diff_sub_gain[j].reshape(1, 2 * HEAD_DIM)
            dims = dict(nq=D // LANES, nk=D // LANES, nv=D)
            q, k, v, kf, vf = _qkv_proj(x, mod_l, norm_mix_g[i], w, diff_q_gain[j], diff_k_gain[j], None,
                                        latent=False, **dims)
            new_kv["diff_k"] = kf.reshape(BATCH, 1, SEQ, DIFF_HEADS, 2, HEAD_DIM)
            new_kv["diff_v"] = vf.reshape(BATCH, 1, SEQ, DIFF_HEADS, 2 * HEAD_DIM)
            x = _diff_attention(x, mod_l, q, k.reshape(BATCH, SEQ, D), v.reshape(BATCH, SEQ, D), None,
                                lam_vecs, sub_gain, w_out, latent=False, lam_init=lam_init)
            q, k, v = _qkv_proj(x, mod_l, norm_mix_g[i], w, diff_q_gain[j], diff_k_gain[j], rope_tabs,
                                latent=True, **dims)
            kc = cache_diff_k[:, j].reshape(DEC_BATCH, PAST_LEN, D).astype(BF16)
            vc = cache_diff_v[:, j].reshape(DEC_BATCH, PAST_LEN, D).astype(BF16)
            x = _diff_attention(x, mod_l, q, k.reshape(DEC_BATCH, DEC_SEQ, D), v.reshape(DEC_BATCH, DEC_SEQ, D),
                                (kc, vc), lam_vecs, sub_gain, w_out, latent=True, lam_init=lam_init)
        x = _moe_layer(x, mod_l, norm_ffn_g[i], i, moe_w_router[i], moe_b_router[i], moe_w_gate_up,
                       bgu_all, moe_w_down, moe_b_down, split=i == DEPTH - 1)
    y_prompt, y_sample = x
    return (y_prompt.reshape(BATCH, SEQ, D), y_sample.reshape(DEC_BATCH, DEC_SEQ, D),
            new_kv["swa_k"], new_kv["swa_v"], new_kv["diff_k"], new_kv["diff_v"])
```

```python
import functools
import math

import jax
import jax.numpy as jnp
from jax import lax
from jax.experimental import pallas as pl
from jax.experimental.pallas import tpu as pltpu

F32 = jnp.float32
BF16 = jnp.bfloat16
I32 = jnp.int32

D = 1024
BATCH = 32
SEQ = 256
DEC_BATCH = 4
DEC_SEQ = 2048
PAST_LEN = 512
DEPTH = 4
GRID_W = 64
HEAD_DIM = 64
SWA_Q_HEADS = 16
SWA_KV_HEADS = 4
SWA_GROUP = SWA_Q_HEADS // SWA_KV_HEADS
SWA_WINDOW = 128
DIFF_HEADS = 8
N_EXPERTS = 32
TOP_K = 4
D_EXPERT = 1024
SWIGLU_ALPHA = 1.702
SWIGLU_LIMIT = 7.0
ROPE_BASE = 10000.0
EPS = 1e-6

T_CTX = BATCH * SEQ
T_LAT = DEC_BATCH * DEC_SEQ
T = T_CTX + T_LAT
N_COND = 1 + DEC_BATCH
COND_PAD = 8

LANES = 128
SUBLANES = 8
BF16_ROWS = 16
MXU_DIM = 256

TM_CONV = 512
TM_QKV = 512
TM_MOE = 512
TM_EXP = 512
CHUNK = SUBLANES
HALF = D // 2
assert 2 * CHUNK == BF16_ROWS
N_TILE_MOE = T // TM_MOE
LOCAL_ROWS = TM_MOE * TOP_K + N_EXPERTS * (CHUNK - 1)
LOCAL_ROWS = -(-LOCAL_ROWS // LANES) * LANES
LOCAL_CHUNKS = LOCAL_ROWS // CHUNK
EXP_CHUNKS = TM_EXP // CHUNK
MAX_GROUP_ROWS = T * TOP_K + N_TILE_MOE * N_EXPERTS * (CHUNK - 1)
MAX_EXP_TILES = -(-MAX_GROUP_ROWS // TM_EXP) + N_EXPERTS
MAX_EXP_CHUNKS = MAX_EXP_TILES * EXP_CHUNKS
N_XBUF = 3
N_YBUF = 3
TABLE_RUN = 8


def _chunks_to_storage(rows):
    r3 = rows.reshape(rows.shape[0] // CHUNK, CHUNK, D)
    return jnp.concatenate([r3[:, :, :HALF], r3[:, :, HALF:]], axis=1).astype(BF16)


def _chunks_from_storage(stored):
    s = stored.astype(F32)
    r3 = jnp.concatenate([s[:, :CHUNK, :], s[:, CHUNK:, :]], axis=-1)
    return r3.reshape(stored.shape[0] * CHUNK, D).astype(BF16)

VMEM_LIMIT = 56 * 1024 * 1024


def _cparams(sem, vmem=VMEM_LIMIT):
    return pltpu.CompilerParams(dimension_semantics=sem, vmem_limit_bytes=vmem)


def _dot(a, b):
    return jnp.dot(a, b, preferred_element_type=F32)


def _dot_nt(a, b):
    return lax.dot_general(a, b, (((1,), (1,)), ((), ())), preferred_element_type=F32)


def _dot_tn(a, b):
    return lax.dot_general(a, b, (((0,), (0,)), ((), ())), preferred_element_type=F32)


def _norm_mod(x, g, scale, shift):
    ms = jnp.mean(x * x, axis=-1, keepdims=True)
    y = x * lax.rsqrt(ms + EPS)
    return (y * g) * (1.0 + scale) + shift


MOD_TN = 1536


def _mod_body(cond_ref, w_ref, b_ref, o_ref):
    c = cond_ref[...]
    s = (c * jax.nn.sigmoid(c)).astype(BF16)
    o_ref[0] = _dot(s, w_ref[0].astype(BF16)) + b_ref[0]


def _modulation(cond, ada_w, ada_b):
    n_layer = ada_w.shape[0]
    n_out = ada_w.shape[2]
    return pl.pallas_call(
        _mod_body,
        out_shape=jax.ShapeDtypeStruct((n_layer, COND_PAD, n_out), F32),
        grid=(n_layer, n_out // MOD_TN),
        in_specs=[
            pl.BlockSpec((COND_PAD, D), lambda i, j: (0, 0)),
            pl.BlockSpec((1, D, MOD_TN), lambda i, j: (i, 0, j)),
            pl.BlockSpec((1, 1, MOD_TN), lambda i, j: (i, 0, j)),
        ],
        out_specs=pl.BlockSpec((1, COND_PAD, MOD_TN), lambda i, j: (i, 0, j)),
        compiler_params=_cparams(("arbitrary", "arbitrary")),
        name="modulation",
    )(cond, ada_w, ada_b.reshape(n_layer, 1, n_out))


def _cond_row(row_tile, tile_rows):
    ctx_tiles = T_CTX // tile_rows
    per_batch = DEC_SEQ // tile_rows
    return jnp.where(row_tile < ctx_tiles, 0, 1 + (row_tile - ctx_tiles) // per_batch)


def _conv_body(*refs, split):
    t = pl.program_id(0)
    if split:
        xc_ref, xl_ref, xcp_ref, xlp_ref, xcn_ref, xln_ref = refs[:6]
        refs = refs[6:]
        is_ctx = t < T_CTX // xc_ref.shape[0]
        x = jnp.where(is_ctx, xc_ref[...], xl_ref[...])
        x_before = jnp.where(is_ctx, xcp_ref[...], xlp_ref[...])
        x_after = jnp.where(is_ctx, xcn_ref[...], xln_ref[...])
    else:
        x_ref, xp_ref, xn_ref = refs[:3]
        refs = refs[3:]
        x, x_before, x_after = x_ref[...], xp_ref[...], xn_ref[...]
    mod_ref, g_ref, win_ref, cw_ref, wout_ref, o_ref = refs
    tm = x.shape[0]
    m = mod_ref[...]
    shift, scale, gate = m[0:1], m[1:2], m[2:3]
    x_ext = jnp.concatenate([x_before, x, x_after], axis=0)
    h = _norm_mod(x_ext, g_ref[...], scale, shift).astype(BF16)
    proj = _dot(h, win_ref[...])
    gate_b = proj[SUBLANES:SUBLANES + tm, :D]
    u_ext = proj[:, D:2 * D] * proj[:, 2 * D:]
    n_ext = tm + 2 * SUBLANES
    u = u_ext[SUBLANES:SUBLANES + tm]
    u_prev = pltpu.roll(u_ext, 1, axis=0)[SUBLANES:SUBLANES + tm]
    u_next = pltpu.roll(u_ext, n_ext - 1, axis=0)[SUBLANES:SUBLANES + tm]
    row = lax.broadcasted_iota(I32, (tm, 1), 0)
    grow = t * tm + row
    seq_mask = jnp.where(grow < T_CTX, SEQ - 1, DEC_SEQ - 1)
    pos = grow & seq_mask
    u_prev = jnp.where(pos == 0, 0.0, u_prev)
    u_next = jnp.where(pos == seq_mask, 0.0, u_next)
    cw = cw_ref[...]
    conv = u_prev * cw[0:1] + u * cw[1:2] + u_next * cw[2:3]
    y = (gate_b * conv).astype(BF16)
    o_ref[...] = x + gate * _dot(y, wout_ref[...])


def _conv_layer(xs, mod_l, norm_g, w_in, conv_w, w_out):
    tm = TM_CONV
    n_tiles = T // tm
    sub_per_tile = tm // SUBLANES
    split = isinstance(xs, tuple)
    if split:
        x_specs, x_args = [], []
        for kind in ("tile", "before", "after"):
            for x_part, tile0 in zip(xs, (0, T_CTX // tm)):
                part_tiles = x_part.shape[0] // tm
                last_sub = x_part.shape[0] // SUBLANES - 1
                if kind == "tile":
                    spec = pl.BlockSpec((tm, D), functools.partial(
                        lambda t, tile0, part_tiles: (jnp.clip(t - tile0, 0, part_tiles - 1), 0),
                        tile0=tile0, part_tiles=part_tiles))
                else:
                    shift = -1 if kind == "before" else sub_per_tile
                    spec = pl.BlockSpec((SUBLANES, D), functools.partial(
                        lambda t, tile0, last_sub, shift: (jnp.clip((t - tile0) * sub_per_tile + shift, 0, last_sub), 0),
                        tile0=tile0, last_sub=last_sub, shift=shift))
                x_specs.append(spec)
                x_args.append(x_part)
    else:
        last_sub = T // SUBLANES - 1
        x_specs = [
            pl.BlockSpec((tm, D), lambda t: (t, 0)),
            pl.BlockSpec((SUBLANES, D), lambda t: (jnp.maximum(t * sub_per_tile - 1, 0), 0)),
            pl.BlockSpec((SUBLANES, D), lambda t: (jnp.minimum((t + 1) * sub_per_tile, last_sub), 0)),
        ]
        x_args = [xs, xs, xs]
    return pl.pallas_call(
        functools.partial(_conv_body, split=split),
        out_shape=jax.ShapeDtypeStruct((T, D), F32),
        grid=(n_tiles,),
        in_specs=x_specs + [
            pl.BlockSpec((None, 6, D), lambda t: (_cond_row(t, tm), 0, 0)),
            pl.BlockSpec((1, D), lambda t: (0, 0)),
            pl.BlockSpec((D, 3 * D), lambda t: (0, 0)),
            pl.BlockSpec((3, D), lambda t: (0, 0)),
            pl.BlockSpec((D, D), lambda t: (0, 0)),
        ],
        out_specs=pl.BlockSpec((tm, D), lambda t: (t, 0)),
        compiler_params=_cparams(("arbitrary",)),
        name="conv_mixer",
    )(*x_args, mod_l, norm_g.reshape(1, D), w_in, conv_w, w_out)


def _head_norm(blk, gain):
    lane = lax.broadcasted_iota(I32, blk.shape, 1)
    low = lane < HEAD_DIM
    s = blk * blk
    s_lo = jnp.sum(jnp.where(low, s, 0.0), axis=-1, keepdims=True)
    s_hi = jnp.sum(jnp.where(low, 0.0, s), axis=-1, keepdims=True)
    ms = jnp.where(low, s_lo, s_hi) * (1.0 / HEAD_DIM)
    return blk * lax.rsqrt(ms + EPS) * gain


def _rope(blk, cos, sin_signed):
    lane = lax.broadcasted_iota(I32, blk.shape, 1)
    first = (lane & 31) < 16
    partner = jnp.where(first, pltpu.roll(blk, LANES - 16, axis=1), pltpu.roll(blk, 16, axis=1))
    return blk * cos + partner * sin_signed


def _qkv_body(*refs, nq, nk, nv, rope, emit_f32, q_scale):
    refs = list(refs)
    x_ref, mod_ref, g_ref, w_ref, qg_ref, kg_ref = refs[:6]
    refs = refs[6:]
    if rope:
        cos_ref, sin_ref = refs[:2]
        refs = refs[2:]
    q_ref, k_ref, v_ref = refs[:3]
    refs = refs[3:]
    if emit_f32:
        kf_ref, vf_ref = refs[:2]
    x = x_ref[...]
    m = mod_ref[...]
    h = _norm_mod(x, g_ref[...], m[1:2], m[0:1]).astype(BF16)
    proj = _dot(h, w_ref[...])
    if rope:
        cos = cos_ref[...]
        sin = sin_ref[...]
    for j in range(nq):
        blk = _head_norm(proj[:, j * LANES:(j + 1) * LANES], qg_ref[...])
        if rope:
            blk = _rope(blk, cos, sin)
        q_ref[:, j * LANES:(j + 1) * LANES] = (blk * q_scale).astype(BF16)
    off = nq * LANES
    for j in range(nk):
        blk = _head_norm(proj[:, off + j * LANES:off + (j + 1) * LANES], kg_ref[...])
        if emit_f32:
            kf_ref[:, j * LANES:(j + 1) * LANES] = blk
        if rope:
            blk = _rope(blk, cos, sin)
        k_ref[:, j * LANES:(j + 1) * LANES] = blk.astype(BF16)
    off = (nq + nk) * LANES
    v = proj[:, off:off + nv]
    v_ref[...] = v.astype(BF16)
    if emit_f32:
        vf_ref[...] = v


def _qkv_proj(x, mod_l, norm_g, w, q_gain, k_gain, rope_tabs, *, latent, nq, nk, nv):
    tm = TM_QKV
    n_rows = T_LAT if latent else T_CTX
    n_tiles = n_rows // tm
    tile0 = (T_CTX // tm) if latent else 0
    n_proj = (nq + nk) * LANES + nv
    gain2 = lambda g: jnp.concatenate([g, g]).reshape(1, LANES)
    in_specs = [
        pl.BlockSpec((tm, D), lambda t: (t + tile0, 0)),
        pl.BlockSpec((None, 6, D), lambda t: (_cond_row(t + tile0, tm), 0, 0)),
        pl.BlockSpec((1, D), lambda t: (0, 0)),
        pl.BlockSpec((D, n_proj), lambda t: (0, 0)),
        pl.BlockSpec((1, LANES), lambda t: (0, 0)),
        pl.BlockSpec((1, LANES), lambda t: (0, 0)),
    ]
    args = [x, mod_l, norm_g.reshape(1, D), w, gain2(q_gain), gain2(k_gain)]
    if latent:
        per_seq = DEC_SEQ // tm
        in_specs += [pl.BlockSpec((tm, LANES), lambda t: (t % per_seq, 0))] * 2
        args += list(rope_tabs)
    out_shape = [
        jax.ShapeDtypeStruct((n_rows, nq * LANES), BF16),
        jax.ShapeDtypeStruct((n_rows, nk * LANES), BF16),
        jax.ShapeDtypeStruct((n_rows, nv), BF16),
    ]
    out_specs = [
        pl.BlockSpec((tm, nq * LANES), lambda t: (t, 0)),
        pl.BlockSpec((tm, nk * LANES), lambda t: (t, 0)),
        pl.BlockSpec((tm, nv), lambda t: (t, 0)),
    ]
    if not latent:
        out_shape += [jax.ShapeDtypeStruct((n_rows, nk * LANES), F32), jax.ShapeDtypeStruct((n_rows, nv), F32)]
        out_specs += [pl.BlockSpec((tm, nk * LANES), lambda t: (t, 0)), pl.BlockSpec((tm, nv), lambda t: (t, 0))]
    body = functools.partial(_qkv_body, nq=nq, nk=nk, nv=nv, rope=latent, emit_f32=not latent,
                             q_scale=HEAD_DIM ** -0.5)
    return pl.pallas_call(
        body,
        out_shape=out_shape,
        grid=(n_tiles,),
        in_specs=in_specs,
        out_specs=out_specs,
        compiler_params=_cparams(("arbitrary",)),
        name="qkv_latent" if latent else "qkv_context",
    )(*args)


def _rope_tables():
    rows = DEC_SEQ // GRID_W
    row_pos = jnp.repeat(jnp.arange(rows, dtype=F32), GRID_W)
    col_pos = jnp.tile(jnp.arange(GRID_W, dtype=F32), rows)
    half = HEAD_DIM // 2
    inv_freq = ROPE_BASE ** (-jnp.arange(0, half, 2, dtype=F32) / half)
    ang_r = row_pos[:, None] * inv_freq[None, :]
    ang_c = col_pos[:, None] * inv_freq[None, :]
    cr, sr, cc, sc = jnp.cos(ang_r), jnp.sin(ang_r), jnp.cos(ang_c), jnp.sin(ang_c)
    cos64 = jnp.concatenate([cr, cr, cc, cc], axis=-1)
    sin64 = jnp.concatenate([-sr, sr, -sc, sc], axis=-1)
    return jnp.concatenate([cos64, cos64], axis=-1), jnp.concatenate([sin64, sin64], axis=-1)


def _swap_halves(blk):
    return pltpu.roll(blk.astype(F32), HEAD_DIM, axis=1).astype(BF16)


def _kv_operands(kh, k_blk, v_blk):
    lane = lax.broadcasted_iota(I32, k_blk.shape, 1)
    low = lane < HEAD_DIM
    k_swp = _swap_halves(k_blk)
    if kh % 2 == 0:
        k2 = jnp.where(low, k_blk, k_swp)
        v_lo = v_blk
    else:
        k2 = jnp.where(low, k_swp, k_blk)
        v_lo = _swap_halves(v_blk)
    v2 = jnp.where(low, v_lo, jnp.ones_like(v_lo))
    return k2, v2


def _stack_query_heads(q_ref, kh, tq):
    lane = lax.broadcasted_iota(I32, (tq, LANES), 1)
    low = lane < HEAD_DIM
    parts = []
    for pair in range(SWA_GROUP // 2):
        blk = (SWA_GROUP // 2) * kh + pair
        qp = q_ref[:, blk * LANES:(blk + 1) * LANES]
        zero = jnp.zeros_like(qp)
        parts += [jnp.where(low, qp, zero), jnp.where(low, zero, qp)]
    return jnp.concatenate(parts, axis=0)


def _sink_column(sink_ref, kh, tq):
    row = lax.broadcasted_iota(I32, (SWA_GROUP * tq, 1), 0)
    col = jnp.full((SWA_GROUP * tq, 1), sink_ref[SWA_GROUP * kh], F32)
    for g in range(1, SWA_GROUP):
        col = jnp.where(row >= g * tq, sink_ref[SWA_GROUP * kh + g], col)
    return col


def _store_head_group(att_ref, kh, tq, acc, sink_term):
    lane = lax.broadcasted_iota(I32, (tq, LANES), 1)
    low = lane < HEAD_DIM
    den = acc[:, HEAD_DIM:HEAD_DIM + 1] + sink_term
    r = acc / den
    r_hi = pltpu.roll(r, HEAD_DIM, axis=1)
    for pair in range(SWA_GROUP // 2):
        blk = (SWA_GROUP // 2) * kh + pair
        a = r[(2 * pair) * tq:(2 * pair + 1) * tq]
        b = r_hi[(2 * pair + 1) * tq:(2 * pair + 2) * tq]
        att_ref[:, blk * LANES:(blk + 1) * LANES] = jnp.where(low, a, b).astype(BF16)


def _swa_ctx_body(sink_ref, x_ref, mod_ref, q_ref, k_ref, v_ref, wout_ref, o_ref, att_ref):
    tq = q_ref.shape[0]
    for kh in range(SWA_KV_HEADS):
        blk = kh // 2
        k2, v2 = _kv_operands(kh, k_ref[:, blk * LANES:(blk + 1) * LANES], v_ref[:, blk * LANES:(blk + 1) * LANES])
        qs = _stack_query_heads(q_ref, kh, tq)
        sink = _sink_column(sink_ref, kh, tq)
        s = _dot_nt(qs, k2)
        mx = jnp.maximum(jnp.max(s, axis=-1, keepdims=True), sink)
        p = jnp.exp(s - mx)
        acc = _dot(p.astype(BF16), v2)
        _store_head_group(att_ref, kh, tq, acc, jnp.exp(sink - mx))
    gate = mod_ref[2:3, :]
    o_ref[...] = x_ref[...] + gate * _dot(att_ref[...], wout_ref[...])


def _swa_ctx_attention(x, mod_l, q, k, v, sink, w_out):
    nkv = SWA_KV_HEADS * HEAD_DIM
    return pl.pallas_call(
        _swa_ctx_body,
        out_shape=jax.ShapeDtypeStruct((T, D), F32),
        grid=(BATCH,),
        in_specs=[
            pl.BlockSpec(memory_space=pltpu.SMEM),
            pl.BlockSpec((SEQ, D), lambda b: (b, 0)),
            pl.BlockSpec((None, 6, D), lambda b: (0, 0, 0)),
            pl.BlockSpec((SEQ, D), lambda b: (b, 0)),
            pl.BlockSpec((SEQ, nkv), lambda b: (b, 0)),
            pl.BlockSpec((SEQ, nkv), lambda b: (b, 0)),
            pl.BlockSpec((D, D), lambda b: (0, 0)),
        ],
        out_specs=pl.BlockSpec((SEQ, D), lambda b: (b, 0)),
        scratch_shapes=[pltpu.VMEM((SEQ, D), BF16)],
        input_output_aliases={1: 0},
        compiler_params=_cparams(("arbitrary",)),
        name="swa_context_attention",
    )(sink, x, mod_l, q, k, v, w_out)


SWA_TQ = 128
SWA_KWIN = 3 * SWA_TQ


def _swa_lat_body(sink_ref, x_ref, mod_ref, q_ref, k_ref, v_ref, kc_ref, vc_ref, wout_ref, o_ref, att_ref):
    j = pl.program_id(1)
    tq = SWA_TQ
    start = jnp.clip((j - 1) * tq, 0, DEC_SEQ - SWA_KWIN)
    start = pl.multiple_of(start, tq)
    row = lax.broadcasted_iota(I32, (SWA_GROUP * tq, 1), 0)
    q_pos = j * tq + (row & (tq - 1))
    k_pos = start + lax.broadcasted_iota(I32, (1, SWA_KWIN), 1)
    valid = jnp.abs(k_pos - q_pos) <= SWA_WINDOW
    for kh in range(SWA_KV_HEADS):
        blk = kh // 2
        lanes = slice(blk * LANES, (blk + 1) * LANES)
        k2, v2 = _kv_operands(kh, k_ref[pl.ds(start, SWA_KWIN), lanes], v_ref[pl.ds(start, SWA_KWIN), lanes])
        kc2, vc2 = _kv_operands(kh, kc_ref[:, lanes], vc_ref[:, lanes])
        qs = _stack_query_heads(q_ref, kh, tq)
        sink = _sink_column(sink_ref, kh, tq)
        s_lat = jnp.where(valid, _dot_nt(qs, k2), -jnp.inf)
        s_ctx = _dot_nt(qs, kc2)
        mx = jnp.maximum(jnp.max(s_lat, axis=-1, keepdims=True), jnp.max(s_ctx, axis=-1, keepdims=True))
        mx = jnp.maximum(mx, sink)
        p_lat = jnp.exp(s_lat - mx).astype(BF16)
        p_ctx = jnp.exp(s_ctx - mx).astype(BF16)
        acc = _dot(p_lat, v2) + _dot(p_ctx, vc2)
        _store_head_group(att_ref, kh, tq, acc, jnp.exp(sink - mx))
    gate = mod_ref[2:3, :]
    o_ref[...] = x_ref[...] + gate * _dot(att_ref[...], wout_ref[...])


def _swa_lat_attention(x, mod_l, q, k, v, k_cache, v_cache, sink, w_out):
    nkv = SWA_KV_HEADS * HEAD_DIM
    nb = DEC_SEQ // SWA_TQ
    x_tile0 = T_CTX // SWA_TQ
    return pl.pallas_call(
        _swa_lat_body,
        out_shape=jax.ShapeDtypeStruct((T, D), F32),
        grid=(DEC_BATCH, nb),
        in_specs=[
            pl.BlockSpec(memory_space=pltpu.SMEM),
            pl.BlockSpec((SWA_TQ, D), lambda b, j: (x_tile0 + b * nb + j, 0)),
            pl.BlockSpec((None, 6, D), lambda b, j: (1 + b, 0, 0)),
            pl.BlockSpec((SWA_TQ, D), lambda b, j: (b * nb + j, 0)),
            pl.BlockSpec((DEC_SEQ, nkv), lambda b, j: (b, 0)),
            pl.BlockSpec((DEC_SEQ, nkv), lambda b, j: (b, 0)),
            pl.BlockSpec((None, PAST_LEN, nkv), lambda b, j: (b, 0, 0)),
            pl.BlockSpec((None, PAST_LEN, nkv), lambda b, j: (b, 0, 0)),
            pl.BlockSpec((D, D), lambda b, j: (0, 0)),
        ],
        out_specs=pl.BlockSpec((SWA_TQ, D), lambda b, j: (x_tile0 + b * nb + j, 0)),
        scratch_shapes=[pltpu.VMEM((SWA_TQ, D), BF16)],
        input_output_aliases={1: 0},
        compiler_params=_cparams(("arbitrary", "arbitrary")),
        name="swa_latent_attention",
    )(sink, x, mod_l, q, k, v, k_cache, v_cache, w_out)


DIFF_TQ = 256


def _diff_body(x_ref, mod_ref, q_ref, k_ref, v_ref, *rest, lam_init, cached):
    if cached:
        kc_ref, vc_ref = rest[:2]
        rest = rest[2:]
    lam_ref, sg_ref, wout_ref, o_ref, att_ref = rest
    tq = q_ref.shape[0]
    lane = lax.broadcasted_iota(I32, (tq, LANES), 1)
    low = lane < HEAD_DIM
    lv = lam_ref[...]
    lam = (jnp.exp(jnp.sum(lv[0:1] * lv[1:2], axis=-1, keepdims=True))
           - jnp.exp(jnp.sum(lv[2:3] * lv[3:4], axis=-1, keepdims=True)) + lam_init)
    sub_gain = sg_ref[...]
    for hd in range(DIFF_HEADS):
        lanes = slice(hd * LANES, (hd + 1) * LANES)
        qh = q_ref[:, lanes]
        vh = v_ref[:, lanes]
        zero = jnp.zeros_like(qh)
        q2 = jnp.concatenate([jnp.where(low, qh, zero), jnp.where(low, zero, qh)], axis=0)
        s = _dot_nt(q2, k_ref[:, lanes])
        mx = jnp.max(s, axis=-1, keepdims=True)
        if cached:
            s_c = _dot_nt(q2, kc_ref[:, lanes])
            mx = jnp.maximum(mx, jnp.max(s_c, axis=-1, keepdims=True))
        acc = _dot(jnp.exp(s - mx).astype(BF16), jnp.concatenate([vh, jnp.ones_like(vh)], axis=1))
        if cached:
            vch = vc_ref[:, lanes]
            acc += _dot(jnp.exp(s_c - mx).astype(BF16), jnp.concatenate([vch, jnp.ones_like(vch)], axis=1))
        r = acc[:, :LANES] / acc[:, LANES:LANES + 1]
        o = r[:tq] - lam * r[tq:]
        ms = jnp.mean(o * o, axis=-1, keepdims=True)
        o = (o * lax.rsqrt(ms + EPS) * sub_gain) * (1.0 - lam_init)
        att_ref[:, lanes] = o.astype(BF16)
    gate = mod_ref[2:3, :]
    o_ref[...] = x_ref[...] + gate * _dot(att_ref[...], wout_ref[...])


def _diff_attention(x, mod_l, q, k, v, cache, lam_vecs, sub_gain, w_out, *, latent, lam_init):
    tq = DIFF_TQ
    n_keys = k.shape[1]
    kv_map = lambda b, j: (b, 0, 0)
    kv_specs = [pl.BlockSpec((None, n_keys, D), kv_map), pl.BlockSpec((None, n_keys, D), kv_map)]
    kv_args = [k, v]
    if cache is not None:
        kv_specs += [pl.BlockSpec((None, PAST_LEN, D), kv_map), pl.BlockSpec((None, PAST_LEN, D), kv_map)]
        kv_args += list(cache)
    if latent:
        per_batch = DEC_SEQ // tq
        grid = (DEC_BATCH, per_batch)
        x_tile0 = T_CTX // tq
        x_map = lambda b, j: (x_tile0 + b * per_batch + j, 0)
        q_map = lambda b, j: (b * per_batch + j, 0)
        mod_map = lambda b, j: (1 + b, 0, 0)
    else:
        per_batch = SEQ // tq
        grid = (BATCH, per_batch)
        x_map = lambda b, j: (b * per_batch + j, 0)
        q_map = x_map
        mod_map = lambda b, j: (0, 0, 0)
    const2 = lambda b, j: (0, 0)
    return pl.pallas_call(
        functools.partial(_diff_body, lam_init=lam_init, cached=cache is not None),
        out_shape=jax.ShapeDtypeStruct((T, D), F32),
        grid=grid,
        in_specs=[
            pl.BlockSpec((tq, D), x_map),
            pl.BlockSpec((None, 6, D), mod_map),
            pl.BlockSpec((tq, D), q_map),
            *kv_specs,
            pl.BlockSpec((4, HEAD_DIM), const2),
            pl.BlockSpec((1, 2 * HEAD_DIM), const2),
            pl.BlockSpec((D, D), const2),
        ],
        out_specs=pl.BlockSpec((tq, D), x_map),
        scratch_shapes=[pltpu.VMEM((tq, D), BF16)],
        input_output_aliases={0: 0},
        compiler_params=_cparams(("arbitrary", "arbitrary")),
        name="diff_latent_attention" if latent else "diff_context_attention",
    )(x, mod_l, q, *kv_args, lam_vecs, sub_gain, w_out)


def _dispatch_body(x_ref, mod_ref, g_ref, wrt_hi_ref, wrt_lo_ref, brt_ref, before_ref, below_ref,
                   z_ref, pg_ref, meta_ref):
    tm = TM_MOE
    m = mod_ref[...]
    h = _norm_mod(x_ref[...], g_ref[...], m[4:5], m[3:4])
    hb = h.astype(BF16)
    h_lo = (h - hb.astype(F32)).astype(BF16)
    wrt_hi = wrt_hi_ref[...]
    logits_t = _dot_nt(wrt_hi, hb) + _dot_nt(wrt_hi, h_lo) + _dot_nt(wrt_lo_ref[...], hb)
    work = logits_t[:N_EXPERTS] + brt_ref[...]
    sub = lax.broadcasted_iota(I32, (N_EXPERTS, tm), 0)
    picks, vals = [], []
    for _ in range(TOP_K):
        best = jnp.max(work, axis=0, keepdims=True)
        idx = jnp.min(jnp.where(work == best, sub, N_EXPERTS), axis=0, keepdims=True)
        hit = sub == idx
        work = jnp.where(hit, -jnp.inf, work)
        picks.append(hit)
        vals.append(best)
    exps = [jnp.exp(v - vals[0]) for v in vals]
    den = exps[0] + exps[1] + exps[2] + exps[3]
    gates = [e / den for e in exps]
    chosen_b = jnp.where(picks[0] | picks[1] | picks[2] | picks[3], 1.0, 0.0).astype(BF16)

    rank_t = _dot(chosen_b, before_ref[...])
    count_c = _dot(chosen_b, jnp.ones((tm, LANES), BF16))
    chunks_c = jnp.floor((count_c + (CHUNK - 1)) * (1.0 / CHUNK))
    chunks_pad = jnp.concatenate([chunks_c, jnp.zeros((LANES - N_EXPERTS, LANES), F32)], axis=0)
    first_c = _dot(below_ref[...], chunks_pad.astype(BF16))[:N_EXPERTS]
    pos_full_t = jnp.concatenate([first_c] * (tm // LANES), axis=1) * CHUNK + rank_t
    pos_rows = [jnp.sum(jnp.where(p, pos_full_t, 0.0), axis=0, keepdims=True) for p in picks]

    row_id = lax.broadcasted_iota(I32, (LOCAL_ROWS, tm), 0).astype(F32)
    sel_t = jnp.zeros((LOCAL_ROWS, tm), F32)
    for k in range(TOP_K):
        sel_t = jnp.where(row_id == pos_rows[k], 1.0, sel_t)
    rows = _dot(sel_t.astype(BF16), hb)
    z_ref[...] = _chunks_to_storage(rows)

    sub_l = lax.broadcasted_iota(I32, (LANES, tm), 0)
    pg_t = jnp.zeros((LANES, tm), F32)
    for k in range(TOP_K):
        pg_t = jnp.where(sub_l == k, pos_rows[k], pg_t)
        pg_t = jnp.where(sub_l == TOP_K + k, gates[k], pg_t)
    pg_ref[...] = pg_t.T

    diag = (lax.broadcasted_iota(I32, (N_EXPERTS, LANES), 0) == lax.broadcasted_iota(I32, (N_EXPERTS, LANES), 1))
    cnt_row = jnp.sum(jnp.where(diag, chunks_c, 0.0), axis=0, keepdims=True)
    first_row = jnp.sum(jnp.where(diag, first_c, 0.0), axis=0, keepdims=True)
    sub8 = lax.broadcasted_iota(I32, (SUBLANES, LANES), 0)
    meta = jnp.where(sub8 == 0, cnt_row, jnp.where(sub8 == 1, first_row, 0.0))
    meta_ref[0] = meta.astype(I32)


def _moe_dispatch(x, mod_l, norm_g, w_router, b_router):
    wrt = jnp.pad(w_router.T, ((0, LANES - N_EXPERTS), (0, 0)))
    wrt_hi = wrt.astype(BF16)
    wrt_lo = (wrt - wrt_hi.astype(F32)).astype(BF16)
    brt = b_router.reshape(N_EXPERTS, 1)
    r = jnp.arange(TM_MOE)
    before = (r[:, None] < r[None, :]).astype(BF16)
    c = jnp.arange(LANES)
    below = (c[None, :] < c[:, None]).astype(BF16)
    const2 = lambda t: (0, 0)
    return pl.pallas_call(
        _dispatch_body,
        out_shape=[
            jax.ShapeDtypeStruct((N_TILE_MOE * LOCAL_CHUNKS, 2 * CHUNK, HALF), BF16),
            jax.ShapeDtypeStruct((T, LANES), F32),
            jax.ShapeDtypeStruct((N_TILE_MOE, SUBLANES, LANES), I32),
        ],
        grid=(N_TILE_MOE,),
        in_specs=[
            pl.BlockSpec((TM_MOE, D), lambda t: (t, 0)),
            pl.BlockSpec((None, 6, D), lambda t: (_cond_row(t, TM_MOE), 0, 0)),
            pl.BlockSpec((1, D), const2),
            pl.BlockSpec((LANES, D), const2),
            pl.BlockSpec((LANES, D), const2),
            pl.BlockSpec((N_EXPERTS, 1), const2),
            pl.BlockSpec((TM_MOE, TM_MOE), const2),
            pl.BlockSpec((LANES, LANES), const2),
        ],
        out_specs=[
            pl.BlockSpec((LOCAL_CHUNKS, 2 * CHUNK, HALF), lambda t: (t, 0, 0)),
            pl.BlockSpec((TM_MOE, LANES), lambda t: (t, 0)),
            pl.BlockSpec((1, SUBLANES, LANES), lambda t: (t, 0, 0)),
        ],
        compiler_params=_cparams(("arbitrary",)),
        name="moe_dispatch",
    )(x, mod_l, norm_g.reshape(1, D), wrt_hi, wrt_lo, brt, before, below)


def _expert_body(cnt_ref, first_ref, tstart_ref, ntile_ref, z_hbm, wgu_ref, bgu_ref, wd_ref, bd_ref, perm_ref,
                 y_hbm, wgu_s, wd_s, xbuf, ybuf, src_tab, sem_in, sem_out):
    e = pl.program_id(0)
    t0 = tstart_ref[e]
    nt = ntile_ref[e]
    total = tstart_ref[N_EXPERTS - 1] + ntile_ref[N_EXPERTS - 1]

    def build_source_table():
        def clear(p, carry):
            src_tab[p] = 0
            return carry

        def per_expert(ex, carry):
            def per_token_tile(tt, pos):
                n = cnt_ref[tt * N_EXPERTS + ex]
                first = tt * LOCAL_CHUNKS + first_ref[tt * N_EXPERTS + ex]
                for c in range(TABLE_RUN):
                    src_tab[pos + c] = first + c

                def per_chunk(c, inner):
                    src_tab[pos + c] = first + c
                    return inner

                lax.fori_loop(TABLE_RUN, n, per_chunk, 0)
                return pos + n

            pos = lax.fori_loop(0, N_TILE_MOE, per_token_tile, tstart_ref[ex] * EXP_CHUNKS)
            lax.fori_loop(pos, (tstart_ref[ex] + ntile_ref[ex]) * EXP_CHUNKS, clear, 0)
            return carry

        lax.fori_loop(0, N_EXPERTS, per_expert, 0)

    def start_gather(g, slot):
        base = jnp.minimum(g, total - 1) * EXP_CHUNKS
        for c in range(EXP_CHUNKS):
            pltpu.make_async_copy(z_hbm.at[src_tab[base + c]], xbuf.at[slot, c], sem_in.at[slot]).start(priority=1)

    def wait_gather(slot):
        pltpu.make_async_copy(z_hbm.at[pl.ds(0, EXP_CHUNKS)], xbuf.at[slot], sem_in.at[slot]).wait()

    def out_copy(g, slot):
        return pltpu.make_async_copy(ybuf.at[slot], y_hbm.at[pl.ds(g * EXP_CHUNKS, EXP_CHUNKS)],
                                     sem_out.at[slot])

    @pl.when(e == 0)
    def _():
        build_source_table()
        for g in range(N_XBUF - 1):
            start_gather(g, g)

    @pl.when(nt > 0)
    def _():
        for blk in range(2 * D_EXPERT // MXU_DIM):
            cols = slice(blk * MXU_DIM, (blk + 1) * MXU_DIM)
            wgu_s[:, cols] = _dot(wgu_ref[:, cols].astype(BF16), perm_ref[...]).astype(BF16)
        wd_s[...] = wd_ref[...].astype(BF16)

    def tile_body(i, carry):
        g = t0 + i
        slot = lax.rem(g, N_XBUF)
        oslot = lax.rem(g, 2)
        wait_gather(slot)
        x = _chunks_from_storage(xbuf[slot])
        acts = []
        for blk in range(2 * D_EXPERT // MXU_DIM):
            cols = slice(blk * MXU_DIM, (blk + 1) * MXU_DIM)
            gu = _dot(x, wgu_s[:, cols]) + bgu_ref[:, cols]
            x_glu = jnp.minimum(gu[:, :LANES], SWIGLU_LIMIT)
            x_lin = jnp.clip(gu[:, LANES:], -SWIGLU_LIMIT, SWIGLU_LIMIT)
            acts.append((x_glu * jax.nn.sigmoid(SWIGLU_ALPHA * x_glu) * (x_lin + 1.0)).astype(BF16))
        y = _dot(jnp.concatenate(acts, axis=1), wd_s[...]) + bd_ref[...]
        start_gather(g + N_XBUF - 1, lax.rem(g + N_XBUF - 1, N_XBUF))

        @pl.when(g >= 2)
        def _():
            out_copy(g - 2, oslot).wait()

        ybuf[oslot] = _chunks_to_storage(y)
        out_copy(g, oslot).start()
        return carry

    lax.fori_loop(0, nt, tile_body, 0)

    @pl.when(e == N_EXPERTS - 1)
    def _():
        for ahead in range(N_XBUF - 1):
            wait_gather(lax.rem(total + ahead, N_XBUF))
        last = total - 1
        out_copy(last, lax.rem(last, 2)).wait()

        @pl.when(total >= 2)
        def _():
            out_copy(last - 1, lax.rem(last - 1, 2)).wait()

        ybuf[0] = jnp.zeros((EXP_CHUNKS, 2 * CHUNK, HALF), BF16)

        def start_fill(g, carry):
            out_copy(g, 0).start()
            return carry

        def wait_fill(g, carry):
            out_copy(g, 0).wait()
            return carry

        lax.fori_loop(total, MAX_EXP_TILES, start_fill, 0)
        lax.fori_loop(total, MAX_EXP_TILES, wait_fill, 0)


def _moe_experts(z, cnt, first, tstart, ntile, layer, w_gate_up, bgu_all, w_down, b_down):
    n_gu = 2 * D_EXPERT
    i = jnp.arange(MXU_DIM)
    perm = (i[:, None] == jnp.where(i < LANES, 2 * i, 2 * (i - LANES) + 1)[None, :]).astype(BF16)
    wmap = lambda e, *_: (layer, e, 0, 0)
    grid_spec = pltpu.PrefetchScalarGridSpec(
        num_scalar_prefetch=4,
        grid=(N_EXPERTS,),
        in_specs=[
            pl.BlockSpec(memory_space=pl.ANY),
            pl.BlockSpec((None, None, D, n_gu), wmap),
            pl.BlockSpec((None, None, 1, n_gu), wmap),
            pl.BlockSpec((None, None, D_EXPERT, D), wmap),
            pl.BlockSpec((None, None, 1, D), wmap),
            pl.BlockSpec((MXU_DIM, MXU_DIM), lambda e, *_: (0, 0)),
        ],
        out_specs=pl.BlockSpec(memory_space=pl.ANY),
        scratch_shapes=[
            pltpu.VMEM((D, n_gu), BF16),
            pltpu.VMEM((D_EXPERT, D), BF16),
            pltpu.VMEM((N_XBUF, EXP_CHUNKS, 2 * CHUNK, HALF), BF16),
            pltpu.VMEM((2, EXP_CHUNKS, 2 * CHUNK, HALF), BF16),
            pltpu.SMEM((MAX_EXP_CHUNKS + TABLE_RUN,), I32),
            pltpu.SemaphoreType.DMA((N_XBUF,)),
            pltpu.SemaphoreType.DMA((2,)),
        ],
    )
    return pl.pallas_call(
        _expert_body,
        out_shape=jax.ShapeDtypeStruct((MAX_EXP_CHUNKS, 2 * CHUNK, HALF), BF16),
        grid_spec=grid_spec,
        compiler_params=_cparams(("arbitrary",)),
        name="moe_experts",
    )(cnt, first, tstart, ntile, z, w_gate_up, bgu_all, w_down,
      b_down.reshape(DEPTH, N_EXPERTS, 1, D), perm)


def _combine_body(src_ref, x_ref, mod_ref, pg_ref, y_hbm, *rest, split):
    o_refs, (ybuf, sem) = rest[:-2], rest[-2:]
    t = pl.program_id(0)
    n_t = pl.num_programs(0)
    slot = lax.rem(t, N_YBUF)

    def start_gather(tile, sl):
        base = jnp.minimum(tile, n_t - 1) * LOCAL_CHUNKS
        for c in range(LOCAL_CHUNKS):
            pltpu.make_async_copy(y_hbm.at[src_ref[base + c]], ybuf.at[sl, c], sem.at[sl]).start(priority=c % 2)

    def wait_gather(sl):
        pltpu.make_async_copy(y_hbm.at[pl.ds(0, LOCAL_CHUNKS)], ybuf.at[sl], sem.at[sl]).wait()

    @pl.when(t == 0)
    def _():
        for ahead in range(N_YBUF - 1):
            start_gather(ahead, ahead)

    wait_gather(slot)
    y = _chunks_from_storage(ybuf[slot])
    pg = pg_ref[...]
    col = lax.broadcasted_iota(I32, (TM_MOE, LOCAL_ROWS), 1).astype(F32)
    comb = jnp.zeros((TM_MOE, LOCAL_ROWS), F32)
    for k in range(TOP_K):
        comb = jnp.where(col == pg[:, k:k + 1], pg[:, TOP_K + k:TOP_K + k + 1], comb)
    moe = _dot(comb.astype(BF16), y)
    ahead_tile = t + N_YBUF - 1
    start_gather(ahead_tile, lax.rem(ahead_tile, N_YBUF))
    new_x = x_ref[...] + mod_ref[5:6, :] * moe
    if split:
        ctx_tiles = T_CTX // TM_MOE

        @pl.when(t < ctx_tiles)
        def _():
            o_refs[0][...] = new_x

        @pl.when(t >= ctx_tiles)
        def _():
            o_refs[1][...] = new_x
    else:
        o_refs[0][...] = new_x

    @pl.when(t == n_t - 1)
    def _():
        for ahead in range(1, N_YBUF):
            wait_gather(lax.rem(t + ahead, N_YBUF))


def _moe_combine(x, mod_l, pg, y, src, *, split):
    ctx_tiles = T_CTX // TM_MOE
    if split:
        out_shape = [jax.ShapeDtypeStruct((T_CTX, D), F32), jax.ShapeDtypeStruct((T_LAT, D), F32)]
        out_specs = [
            pl.BlockSpec((TM_MOE, D), lambda t, *_: (jnp.minimum(t, ctx_tiles - 1), 0)),
            pl.BlockSpec((TM_MOE, D), lambda t, *_: (jnp.maximum(t - ctx_tiles, 0), 0)),
        ]
        aliases = {}
    else:
        out_shape = jax.ShapeDtypeStruct((T, D), F32)
        out_specs = pl.BlockSpec((TM_MOE, D), lambda t, *_: (t, 0))
        aliases = {1: 0}
    grid_spec = pltpu.PrefetchScalarGridSpec(
        num_scalar_prefetch=1,
        grid=(N_TILE_MOE,),
        in_specs=[
            pl.BlockSpec((TM_MOE, D), lambda t, *_: (t, 0)),
            pl.BlockSpec((None, 6, D), lambda t, *_: (_cond_row(t, TM_MOE), 0, 0)),
            pl.BlockSpec((TM_MOE, LANES), lambda t, *_: (t, 0)),
            pl.BlockSpec(memory_space=pl.ANY),
        ],
        out_specs=out_specs,
        scratch_shapes=[
            pltpu.VMEM((N_YBUF, LOCAL_CHUNKS, 2 * CHUNK, HALF), BF16),
            pltpu.SemaphoreType.DMA((N_YBUF,)),
        ],
    )
    return pl.pallas_call(
        functools.partial(_combine_body, split=split),
        out_shape=out_shape,
        grid_spec=grid_spec,
        input_output_aliases=aliases,
        compiler_params=_cparams(("arbitrary",)),
        name="moe_combine_final" if split else "moe_combine",
    )(src, x, mod_l, pg, y)


def _moe_layer(x, mod_l, norm_g, layer, w_router, b_router, w_gate_up, bgu_all, w_down, b_down, *, split):
    z, pg, meta = _moe_dispatch(x, mod_l, norm_g, w_router, b_router)
    cnt = meta[:, 0, :N_EXPERTS]
    first = meta[:, 1, :N_EXPERTS]
    ntile = (jnp.sum(cnt, axis=0) + EXP_CHUNKS - 1) // EXP_CHUNKS
    tstart = jnp.cumsum(ntile) - ntile
    y = _moe_experts(z, cnt.reshape(-1), first.reshape(-1), tstart, ntile, layer,
                     w_gate_up, bgu_all, w_down, b_down)
    end = first + cnt
    seen = jnp.cumsum(cnt, axis=0) - cnt
    j = jnp.arange(LOCAL_CHUNKS, dtype=I32)
    e_of_j = jnp.sum((end[:, None, :] <= j[None, :, None]).astype(I32), axis=-1)
    used = j[None, :] < end[:, -1:]
    own = jnp.arange(N_EXPERTS, dtype=I32)[None, None, :] == e_of_j[:, :, None]
    offset = tstart[None, :] * EXP_CHUNKS + seen - first
    src = j[None, :] + jnp.sum(jnp.where(own, offset[:, None, :], 0), axis=-1)
    src = jnp.where(used, src, 0).astype(I32).reshape(-1)
    return _moe_combine(x, mod_l, pg, y, src, split=split)


def kernel(x_prompt, x_sample, c, cache_swa_k, cache_swa_v, cache_diff_k, cache_diff_v, c_ctx, norm_mix_g, norm_ffn_g, ada_w, ada_b, conv_w_in, conv_w, conv_w_out, swa_w_qkv, swa_q_gain, swa_k_gain, swa_sink, swa_w_out, diff_w_qkv, diff_q_gain, diff_k_gain, diff_lam_q1, diff_lam_k1, diff_lam_q2, diff_lam_k2, diff_sub_gain, diff_w_out, moe_w_router, moe_b_router, moe_w_gate_up, moe_b_gate_up, moe_w_down, moe_b_down):
    x = (x_prompt.reshape(T_CTX, D), x_sample.reshape(T_LAT, D))
    cond = jnp.concatenate([c_ctx[None, :], c, jnp.zeros((COND_PAD - N_COND, D), F32)], axis=0)
    mod = _modulation(cond, ada_w, ada_b).reshape(DEPTH, COND_PAD, 6, D)
    rope_tabs = _rope_tables()
    n_gu = 2 * D_EXPERT
    bgu_all = moe_b_gate_up.reshape(DEPTH, N_EXPERTS, n_gu // MXU_DIM, LANES, 2)
    bgu_all = bgu_all.transpose(0, 1, 2, 4, 3).reshape(DEPTH, N_EXPERTS, 1, n_gu)
    new_kv = {}
    for i in range(DEPTH):
        j = i // 3
        kind = i % 3
        mod_l = mod[i]
        if kind == 0:
            x = _conv_layer(x, mod_l, norm_mix_g[i], conv_w_in[j].astype(BF16), conv_w[j],
                            conv_w_out[j].astype(BF16))
        elif kind == 1:
            w = swa_w_qkv[j].astype(BF16)
            w_out = swa_w_out[j].astype(BF16)
            nkv = SWA_KV_HEADS * HEAD_DIM
            dims = dict(nq=SWA_Q_HEADS * HEAD_DIM // LANES, nk=nkv // LANES, nv=nkv)
            q, k, v, kf, vf = _qkv_proj(x, mod_l, norm_mix_g[i], w, swa_q_gain[j], swa_k_gain[j], None,
                                        latent=False, **dims)
            new_kv["swa_k"] = kf.reshape(BATCH, 1, SEQ, SWA_KV_HEADS, HEAD_DIM)
            new_kv["swa_v"] = vf.reshape(BATCH, 1, SEQ, SWA_KV_HEADS, HEAD_DIM)
            x = _swa_ctx_attention(x, mod_l, q, k, v, swa_sink[j], w_out)
            q, k, v = _qkv_proj(x, mod_l, norm_mix_g[i], w, swa_q_gain[j], swa_k_gain[j], rope_tabs,
                                latent=True, **dims)
            kc = cache_swa_k[:, j].reshape(DEC_BATCH, PAST_LEN, nkv).astype(BF16)
            vc = cache_swa_v[:, j].reshape(DEC_BATCH, PAST_LEN, nkv).astype(BF16)
            x = _swa_lat_attention(x, mod_l, q, k, v, kc, vc, swa_sink[j], w_out)
        else:
            w = diff_w_qkv[j].astype(BF16)
            w_out = diff_w_out[j].astype(BF16)
            lam_init = 0.8 - 0.6 * math.exp(-0.3 * i)
            lam_vecs = jnp.stack([diff_lam_q1[j], diff_lam_k1[j], diff_lam_q2[j], diff_lam_k2[j]])
            sub_gain = diff_sub_gain[j].reshape(1, 2 * HEAD_DIM)
            dims = dict(nq=D // LANES, nk=D // LANES, nv=D)
            q, k, v, kf, vf = _qkv_proj(x, mod_l, norm_mix_g[i], w, diff_q_gain[j], diff_k_gain[j], None,
                                        latent=False, **dims)
            new_kv["diff_k"] = kf.reshape(BATCH, 1, SEQ, DIFF_HEADS, 2, HEAD_DIM)
            new_kv["diff_v"] = vf.reshape(BATCH, 1, SEQ, DIFF_HEADS, 2 * HEAD_DIM)
            x = _diff_attention(x, mod_l, q, k.reshape(BATCH, SEQ, D), v.reshape(BATCH, SEQ, D), None,
                                lam_vecs, sub_gain, w_out, latent=False, lam_init=lam_init)
            q, k, v = _qkv_proj(x, mod_l, norm_mix_g[i], w, diff_q_gain[j], diff_k_gain[j], rope_tabs,
                                latent=True, **dims)
            kc = cache_diff_k[:, j].reshape(DEC_BATCH, PAST_LEN, D).astype(BF16)
            vc = cache_diff_v[:, j].reshape(DEC_BATCH, PAST_LEN, D).astype(BF16)
            x = _diff_attention(x, mod_l, q, k.reshape(DEC_BATCH, DEC_SEQ, D), v.reshape(DEC_BATCH, DEC_SEQ, D),
                                (kc, vc), lam_vecs, sub_gain, w_out, latent=True, lam_init=lam_init)
        x = _moe_layer(x, mod_l, norm_ffn_g[i], i, moe_w_router[i], moe_b_router[i], moe_w_gate_up,
                       bgu_all, moe_w_down, moe_b_down, split=i == DEPTH - 1)
    y_prompt, y_sample = x
    return (y_prompt.reshape(BATCH, SEQ, D), y_sample.reshape(DEC_BATCH, DEC_SEQ, D),
            new_kv["swa_k"], new_kv["swa_v"], new_kv["diff_k"], new_kv["diff_v"])
```
